```python
import jax
import jax.numpy as jnp
from jax import lax
import numpy as np

D_MODEL = 1024
BATCH = 32
SEQ = 2048
DEPTH = 2

CHUNK = 64
D_PLE = 256
N_EVEN = (DEPTH + 1) // 2
N_ODD = DEPTH // 2
D_FF = 2816
A_HEADS = 8
A_KV_HEADS = 2
A_GROUP = A_HEADS // A_KV_HEADS
A_HEAD_DIM = 64
A_WIDTH = A_HEADS * A_HEAD_DIM
A_KV_WIDTH = A_KV_HEADS * A_HEAD_DIM
A_WINDOW = 128
A_PREV_CHUNKS = A_WINDOW // CHUNK
B_WIDTH = 512
B_BLOCKS = 8
B_BLOCK = B_WIDTH // B_BLOCKS
B_CONV = 4
RG_C = 8.0
AB_PROJ = A_WIDTH + 2 * A_KV_WIDTH + 2 * B_WIDTH
C_HEADS = 8
C_HEAD_DIM = 128
C_WIDTH = C_HEADS * C_HEAD_DIM
C_CONV = 4
C_PROJ = 4 * C_WIDTH + 2 * C_HEADS
DN_ALPHA = (2.0 * DEPTH) ** 0.25
DN_BETA = (8.0 * DEPTH) ** -0.25
LN_EPS = 1e-5
NORM_EPS = 1e-6
NEG = -1e30

kernel_name = 'hybrid_swa_rglru_gdn_deepnorm_macaron'


def layer_norm(x, g, b):
    xf = x.astype(jnp.float32)
    mu = jnp.mean(xf, -1, keepdims=True)
    var = jnp.mean(jnp.square(xf - mu), -1, keepdims=True)
    return ((xf - mu) * lax.rsqrt(var + LN_EPS) * g + b).astype(x.dtype)


def swiglu(x, wg, wu, wd):
    return (jax.nn.silu(x @ wg) * (x @ wu)) @ wd


def causal_dwconv(x, w):
    k, s = w.shape[0], x.shape[1]
    xp = jnp.pad(x, ((0, 0), (k - 1, 0), (0, 0)))
    y = xp[:, 0:s] * w[0]
    for j in range(1, k):
        y = y + xp[:, j:j + s] * w[j]
    return y


def chunk_band(t, n_prev):
    b, s = t.shape[:2]
    nc = s // CHUNK
    pad = n_prev * CHUNK
    tp = jnp.pad(t, ((0, 0), (pad, 0), (0, 0), (0, 0)))
    return jnp.concatenate(
        [tp[:, j * CHUNK:j * CHUNK + s].reshape(b, nc, CHUNK, *t.shape[2:]) for j in range(n_prev + 1)],
        axis=2)


def alibi_slopes(n):
    return 2.0 ** (-8.0 * jnp.arange(1, n + 1, dtype=jnp.float32) / n)


def sliding_window_sink_attention(q, k, v, sinks):
    b, s = q.shape[:2]
    nc = s // CHUNK
    pad = A_PREV_CHUNKS * CHUNK
    nk = pad + CHUNK
    qb = q.reshape(b, nc, CHUNK, A_KV_HEADS, A_GROUP, A_HEAD_DIM)
    kb = chunk_band(k, A_PREV_CHUNKS)
    vb = chunk_band(v, A_PREV_CHUNKS)
    sc = jnp.einsum('bnckgd,bnskd->bnkgcs', qb, kb).astype(jnp.float32) * (A_HEAD_DIM ** -0.5)
    dist = jnp.abs(jnp.arange(CHUNK)[:, None] + pad - jnp.arange(nk)[None, :]).astype(jnp.float32)
    slopes = alibi_slopes(A_HEADS).reshape(A_KV_HEADS, A_GROUP)
    valid = (jnp.arange(nc)[:, None] * CHUNK + jnp.arange(nk)[None, :] - pad) >= 0
    sc = sc - slopes[:, :, None, None] * dist
    sc = jnp.where(valid[None, :, None, None, None, :], sc, NEG)
    sink = sinks.astype(jnp.float32).reshape(A_KV_HEADS, A_GROUP)[:, :, None]
    m = jnp.maximum(sc.max(-1), sink)
    pr = jnp.exp(sc - m[..., None])
    den = pr.sum(-1) + jnp.exp(sink - m)
    o = jnp.einsum('bnkgcs,bnskd->bnckgd', pr / den[..., None], vb.astype(jnp.float32))
    return o.reshape(b, s, A_WIDTH).astype(q.dtype)


def rg_lru(x, w_a, b_a, w_x, b_x, lam):
    xb = x.reshape(*x.shape[:2], B_BLOCKS, B_BLOCK)
    r = jax.nn.sigmoid(jnp.einsum('bshi,hij->bshj', xb, w_a).reshape(x.shape) + b_a)
    i = jax.nn.sigmoid(jnp.einsum('bshi,hij->bshj', xb, w_x).reshape(x.shape) + b_x)
    log_a = (-RG_C * r * jax.nn.softplus(-lam)).astype(jnp.float32)
    a = jnp.exp(log_a)
    u = jnp.sqrt(-jnp.expm1(2.0 * log_a)) * (i * x).astype(jnp.float32)

    def combine(c1, c2):
        a1, b1 = c1
        a2, b2 = c2
        return a1 * a2, a2 * b1 + b2

    _, h = lax.associative_scan(combine, (a, u), axis=1)
    return h.astype(x.dtype)


def mixer_ab(x, w_in, sinks, conv_w, conv_b, w_a, b_a, w_x, b_x, lam, w_out):
    b, s = x.shape[:2]
    proj = x @ w_in
    o1 = A_WIDTH
    o2 = o1 + A_KV_WIDTH
    o3 = o2 + A_KV_WIDTH
    o4 = o3 + B_WIDTH
    q = proj[..., :o1].reshape(b, s, A_HEADS, A_HEAD_DIM)
    k = proj[..., o1:o2].reshape(b, s, A_KV_HEADS, A_HEAD_DIM)
    v = proj[..., o2:o3].reshape(b, s, A_KV_HEADS, A_HEAD_DIM)
    bx = proj[..., o3:o4]
    bg = proj[..., o4:]
    ya = sliding_window_sink_attention(q, k, v, sinks)
    bx = causal_dwconv(bx, conv_w) + conv_b
    yb = rg_lru(bx, w_a, b_a, w_x, b_x, lam) * jax.nn.gelu(bg)
    return jnp.concatenate([ya, yb], axis=-1) @ w_out


def gated_delta_rule(q, k, v, g, beta):
    f32 = jnp.float32
    b, s, h, dk = q.shape
    dv = v.shape[-1]
    nc = s // CHUNK

    def to_chunks(t):
        return t.astype(f32).reshape(b, nc, CHUNK, h, -1).transpose(1, 0, 3, 2, 4)

    q = to_chunks(q) * (dk ** -0.5)
    k = to_chunks(k)
    v = to_chunks(v)
    g = g.astype(f32).reshape(b, nc, CHUNK, h).transpose(1, 0, 3, 2)
    beta = beta.astype(f32).reshape(b, nc, CHUNK, h).transpose(1, 0, 3, 2)
    gc = jnp.cumsum(g, axis=-1)
    tril = jnp.tril(jnp.ones((CHUNK, CHUNK), bool))
    strict = jnp.tril(jnp.ones((CHUNK, CHUNK), bool), -1)
    diff = gc[..., :, None] - gc[..., None, :]
    decay = jnp.where(tril, jnp.exp(jnp.where(tril, diff, 0.0)), 0.0)
    kb = k * beta[..., None]
    lmat = jnp.where(strict, jnp.einsum('nbhid,nbhjd->nbhij', kb, k) * decay, 0.0)
    amat = lmat + jnp.eye(CHUNK, dtype=f32)
    u = lax.linalg.triangular_solve(amat, v * beta[..., None], left_side=True, lower=True, unit_diagonal=True)
    w = lax.linalg.triangular_solve(amat, kb * jnp.exp(gc)[..., None], left_side=True, lower=True, unit_diagonal=True)
    attn = jnp.einsum('nbhid,nbhjd->nbhij', q, k) * decay
    qg = q * jnp.exp(gc)[..., None]
    kdec = k * jnp.exp(gc[..., -1:] - gc)[..., None]
    glast = jnp.exp(gc[..., -1])

    def step(state, xs):
        qg_n, kdec_n, w_n, u_n, attn_n, gl_n = xs
        v_new = u_n - jnp.einsum('bhcd,bhde->bhce', w_n, state)
        o = jnp.einsum('bhcd,bhde->bhce', qg_n, state) + jnp.einsum('bhij,bhje->bhie', attn_n, v_new)
        state = state * gl_n[..., None, None] + jnp.einsum('bhcd,bhce->bhde', kdec_n, v_new)
        return state, o

    s0 = jnp.zeros((b, h, dk, dv), f32)
    _, o = lax.scan(step, s0, (qg, kdec, w, u, attn, glast))
    return o.transpose(1, 0, 3, 2, 4).reshape(b, s, h, dv)


def mixer_c(x, w_in, conv_w, a_log, dt_bias, norm_g, w_out):
    b, s = x.shape[:2]
    proj = x @ w_in
    qkv = jax.nn.silu(causal_dwconv(proj[..., :3 * C_WIDTH], conv_w))
    z = proj[..., 3 * C_WIDTH:4 * C_WIDTH].reshape(b, s, C_HEADS, C_HEAD_DIM)
    b_logit = proj[..., 4 * C_WIDTH:4 * C_WIDTH + C_HEADS]
    a_in = proj[..., 4 * C_WIDTH + C_HEADS:]
    q = qkv[..., :C_WIDTH].reshape(b, s, C_HEADS, C_HEAD_DIM).astype(jnp.float32)
    k = qkv[..., C_WIDTH:2 * C_WIDTH].reshape(b, s, C_HEADS, C_HEAD_DIM).astype(jnp.float32)
    v = qkv[..., 2 * C_WIDTH:].reshape(b, s, C_HEADS, C_HEAD_DIM)
    q = q * lax.rsqrt(jnp.sum(q * q, -1, keepdims=True) + NORM_EPS)
    k = k * lax.rsqrt(jnp.sum(k * k, -1, keepdims=True) + NORM_EPS)
    beta = jax.nn.sigmoid(b_logit.astype(jnp.float32))
    g = -jnp.exp(a_log.astype(jnp.float32)) * jax.nn.softplus((a_in + dt_bias).astype(jnp.float32))
    o = gated_delta_rule(q, k, v, g, beta)
    o = o * lax.rsqrt(jnp.mean(o * o, -1, keepdims=True) + NORM_EPS) * norm_g
    o = (o * jax.nn.silu(z.astype(jnp.float32))).astype(x.dtype)
    return o.reshape(b, s, C_WIDTH) @ w_out


def setup_inputs(seed: int = 0) -> dict:
    key = jax.random.key(seed)
    ks = iter(jax.random.split(key, 48))
    f32 = jnp.float32

    def nrm(shape, scale):
        return jax.random.normal(next(ks), shape, f32) * scale

    d = D_MODEL
    x = nrm((BATCH, SEQ, d), 1.0)
    p = nrm((DEPTH, BATCH, SEQ, D_PLE), 1.0)
    ffn1_wg = nrm((DEPTH, d, D_FF), d ** -0.5)
    ffn1_wu = nrm((DEPTH, d, D_FF), d ** -0.5)
    ffn1_wd = nrm((DEPTH, D_FF, d), DN_BETA * D_FF ** -0.5)
    ffn2_wg = nrm((DEPTH, d, D_FF), d ** -0.5)
    ffn2_wu = nrm((DEPTH, d, D_FF), d ** -0.5)
    ffn2_wd = nrm((DEPTH, D_FF, d), DN_BETA * D_FF ** -0.5)
    ln_g = 1.0 + nrm((DEPTH, 3, d), 0.02)
    ln_b = nrm((DEPTH, 3, d), 0.02)
    ple_wg = nrm((DEPTH, d, d), d ** -0.5)
    ple_bg = nrm((DEPTH, d), 0.02)
    ple_wp = nrm((DEPTH, D_PLE, d), D_PLE ** -0.5)
    ab_w_in = nrm((N_EVEN, d, AB_PROJ), d ** -0.5)
    a_sinks = nrm((N_EVEN, A_HEADS), 0.5)
    b_conv_w = nrm((N_EVEN, B_CONV, B_WIDTH), B_CONV ** -0.5)
    b_conv_b = nrm((N_EVEN, B_WIDTH), 0.02)
    b_wa = nrm((N_EVEN, B_BLOCKS, B_BLOCK, B_BLOCK), B_BLOCK ** -0.5)
    b_ba = nrm((N_EVEN, B_WIDTH), 0.02)
    b_wx = nrm((N_EVEN, B_BLOCKS, B_BLOCK, B_BLOCK), B_BLOCK ** -0.5)
    b_bx = nrm((N_EVEN, B_WIDTH), 0.02)
    a_c = jax.random.uniform(next(ks), (N_EVEN, B_WIDTH), f32, 0.9, 0.999)
    a0 = a_c ** (1.0 / RG_C)
    b_lam = jnp.log(a0) - jnp.log1p(-a0)
    ab_w_out = nrm((N_EVEN, A_WIDTH + B_WIDTH, d), DN_BETA * (A_WIDTH + B_WIDTH) ** -0.5)
    c_w_in = nrm((N_ODD, d, C_PROJ), d ** -0.5)
    c_conv_w = nrm((N_ODD, C_CONV, 3 * C_WIDTH), C_CONV ** -0.5)
    c_a_log = jnp.log(jax.random.uniform(next(ks), (N_ODD, C_HEADS), f32, 1.0, 16.0))
    dt = jnp.exp(jax.random.uniform(next(ks), (N_ODD, C_HEADS), f32, np.log(1e-3), np.log(1e-1)))
    c_dt_bias = dt + jnp.log(-jnp.expm1(-dt))
    c_norm_g = 1.0 + nrm((N_ODD, C_HEAD_DIM), 0.02)
    c_w_out = nrm((N_ODD, C_WIDTH, d), DN_BETA * C_WIDTH ** -0.5)
    return {'x': x, 'p': p,
            'ffn1_wg': ffn1_wg, 'ffn1_wu': ffn1_wu, 'ffn1_wd': ffn1_wd,
            'ffn2_wg': ffn2_wg, 'ffn2_wu': ffn2_wu, 'ffn2_wd': ffn2_wd,
            'ln_g': ln_g, 'ln_b': ln_b,
            'ple_wg': ple_wg, 'ple_bg': ple_bg, 'ple_wp': ple_wp,
            'ab_w_in': ab_w_in, 'a_sinks': a_sinks,
            'b_conv_w': b_conv_w, 'b_conv_b': b_conv_b,
            'b_wa': b_wa, 'b_ba': b_ba, 'b_wx': b_wx, 'b_bx': b_bx, 'b_lam': b_lam,
            'ab_w_out': ab_w_out,
            'c_w_in': c_w_in, 'c_conv_w': c_conv_w, 'c_a_log': c_a_log, 'c_dt_bias': c_dt_bias,
            'c_norm_g': c_norm_g, 'c_w_out': c_w_out}


def reference(x, p, ffn1_wg, ffn1_wu, ffn1_wd, ffn2_wg, ffn2_wu, ffn2_wd, ln_g, ln_b,
              ple_wg, ple_bg, ple_wp, ab_w_in, a_sinks, b_conv_w, b_conv_b,
              b_wa, b_ba, b_wx, b_bx, b_lam, ab_w_out,
              c_w_in, c_conv_w, c_a_log, c_dt_bias, c_norm_g, c_w_out):
    for i in range(DEPTH):
        j = i // 2
        x = layer_norm(DN_ALPHA * x + 0.5 * swiglu(x, ffn1_wg[i], ffn1_wu[i], ffn1_wd[i]), ln_g[i, 0], ln_b[i, 0])
        if i % 2 == 0:
            y = mixer_ab(x, ab_w_in[j], a_sinks[j], b_conv_w[j], b_conv_b[j],
                         b_wa[j], b_ba[j], b_wx[j], b_bx[j], b_lam[j], ab_w_out[j])
        else:
            y = mixer_c(x, c_w_in[j], c_conv_w[j], c_a_log[j], c_dt_bias[j], c_norm_g[j], c_w_out[j])
        x = layer_norm(DN_ALPHA * x + y, ln_g[i, 1], ln_b[i, 1])
        x = layer_norm(DN_ALPHA * x + 0.5 * swiglu(x, ffn2_wg[i], ffn2_wu[i], ffn2_wd[i]), ln_g[i, 2], ln_b[i, 2])
        x = x + jax.nn.sigmoid(x @ ple_wg[i] + ple_bg[i]) * (p[i] @ ple_wp[i])
    return x
```

```python
import functools

import jax
import jax.numpy as jnp
from jax import lax
from jax.experimental import pallas as pl
from jax.experimental.pallas import tpu as pltpu

F32 = jnp.float32
BF16 = jnp.bfloat16

DEPTH = 2
CHUNK = 64
A_HEADS = 8
A_KV_HEADS = 2
A_HEAD_DIM = 64
A_WIDTH = A_HEADS * A_HEAD_DIM
A_KV_WIDTH = A_KV_HEADS * A_HEAD_DIM
A_PREV_CHUNKS = 2
A_BAND = (A_PREV_CHUNKS + 1) * CHUNK
B_WIDTH = 512
B_BLOCKS = 8
B_CONV = 4
RG_C = 8.0
C_HEADS = 8
C_HEAD_DIM = 128
C_WIDTH = C_HEADS * C_HEAD_DIM
C_CONV = 4
DN_ALPHA = (2.0 * DEPTH) ** 0.25
LN_EPS = 1e-5
NORM_EPS = 1e-6
NEG = -1e30

LANES = 128
SUBLANES = 8
FFN_ROWS = 512
FFN_COLS = 256
MIX_BATCH = 8
VMEM_LIMIT = 56 * 1024 * 1024


def _dot(a, b):
    return jnp.dot(a, b, preferred_element_type=F32)


def _dot_nt(a, b):
    return lax.dot_general(a, b, (((1,), (1,)), ((), ())), preferred_element_type=F32)


def _dot_tn(a, b):
    return lax.dot_general(a, b, (((0,), (0,)), ((), ())), preferred_element_type=F32)


def _layer_norm(z, g, b):
    mu = jnp.mean(z, axis=-1, keepdims=True)
    d = z - mu
    var = jnp.mean(d * d, axis=-1, keepdims=True)
    return d * lax.rsqrt(var + LN_EPS) * g + b


def _silu(x):
    return x * jax.nn.sigmoid(x)


def _softplus(x):
    e = jnp.exp(-jnp.abs(x))
    u = 1.0 + e
    log1p_e = jnp.where(u == 1.0, e, jnp.log(u) * (e / (u - 1.0)))
    return jnp.maximum(x, 0.0) + log1p_e


def _neg_expm1(y):
    return -jnp.tanh(0.5 * y) * (jnp.exp(y) + 1.0)


def _resident(shape):
    zeros = (0,) * len(shape)
    return pl.BlockSpec(shape, lambda *_: zeros, pipeline_mode=pl.Buffered(1))


def _ffn_body(x_ref, wg_ref, wu_ref, wd_ref, lng_ref, lnb_ref, *rest, ple, n_chunks):
    if ple:
        p_ref, pwg_ref, pbg_ref, pwp_ref, o_ref, h_scr = rest
    else:
        o_ref, h_scr = rest
    x = x_ref[...]
    xb = x.astype(BF16)
    for c in range(n_chunks):
        sl = slice(c * FFN_COLS, (c + 1) * FFN_COLS)
        g = _dot(xb, wg_ref[:, sl])
        u = _dot(xb, wu_ref[:, sl])
        h_scr[:, sl] = (_silu(g) * u).astype(BF16)
    y = _dot(h_scr[...], wd_ref[...])
    out = _layer_norm(DN_ALPHA * x + 0.5 * y, lng_ref[...], lnb_ref[...])
    if ple:
        gate = jax.nn.sigmoid(_dot(out.astype(BF16), pwg_ref[...]) + pbg_ref[...])
        out = out + gate * _dot(p_ref[...].astype(BF16), pwp_ref[...])
    o_ref[...] = out


def _ffn(x2, wg, wu, wd, lng, lnb, ple_args=None):
    n, d = x2.shape
    f = wg.shape[1]
    rows = min(FFN_ROWS, n)
    assert n % rows == 0 and f % FFN_COLS == 0
    row_spec = pl.BlockSpec((rows, d), lambda i: (i, 0))
    in_specs = [row_spec, _resident((d, f)), _resident((d, f)), _resident((f, d)),
                _resident((1, d)), _resident((1, d))]
    args = [x2, wg, wu, wd, lng, lnb]
    if ple_args is not None:
        p2, pwg, pbg, pwp = ple_args
        dp = p2.shape[1]
        in_specs += [pl.BlockSpec((rows, dp), lambda i: (i, 0)), _resident((d, d)),
                     _resident((1, d)), _resident((dp, d))]
        args += [p2, pwg, pbg, pwp]
    return pl.pallas_call(
        functools.partial(_ffn_body, ple=ple_args is not None, n_chunks=f // FFN_COLS),
        out_shape=jax.ShapeDtypeStruct((n, d), F32),
        grid=(n // rows,),
        in_specs=in_specs,
        out_specs=row_spec,
        scratch_shapes=[pltpu.VMEM((rows, f), BF16)],
        compiler_params=pltpu.CompilerParams(dimension_semantics=("arbitrary",),
                                             vmem_limit_bytes=VMEM_LIMIT),
        name="ffn_ple" if ple_args is not None else "ffn",
    )(*args)


AB_Q0 = 0
AB_K0 = A_WIDTH
AB_V0 = AB_K0 + 2 * A_KV_WIDTH
AB_X0 = AB_V0 + 2 * A_KV_WIDTH
AB_G0 = AB_X0 + B_WIDTH
AB_COLS = AB_G0 + B_WIDTH


def _mixer_ab_body(sinks_ref, x_ref, win_ref, convw_ref, convb_ref, wa_ref, ba_ref, wx_ref, bx_ref,
                   lam_ref, wout_ref, lng_ref, lnb_ref, o_ref,
                   proj_scr, kbuf, vbuf, xpad, hcarry, y_scr, *, bb):
    n = pl.program_id(1)
    rows = bb * CHUNK
    d_model = x_ref.shape[-1]

    @pl.when(n == 0)
    def _():
        kbuf[...] = jnp.zeros_like(kbuf)
        vbuf[...] = jnp.zeros_like(vbuf)
        xpad[...] = jnp.zeros_like(xpad)
        hcarry[...] = jnp.zeros_like(hcarry)

    x = x_ref[...].reshape(rows, d_model)
    proj_scr[...] = _dot(x.astype(BF16), win_ref[...])

    lo = lax.broadcasted_iota(jnp.int32, (rows, LANES), 1) < A_HEAD_DIM

    def masked_variants(c0):
        ta = proj_scr[:, c0:c0 + LANES]
        tb = proj_scr[:, c0 + LANES:c0 + 2 * LANES]
        zero = jnp.zeros_like(ta)
        parts = [jnp.where(lo, ta, zero), jnp.where(lo, zero, tb),
                 jnp.where(lo, tb, zero), jnp.where(lo, zero, ta)]
        return jnp.concatenate(parts, axis=1).astype(BF16).reshape(bb, CHUNK, 4 * LANES)

    old_k = kbuf[:, CHUNK:A_BAND, :]
    kbuf[:, 0:A_BAND - CHUNK, :] = old_k
    kbuf[:, A_BAND - CHUNK:A_BAND, :] = masked_variants(AB_K0)
    old_v = vbuf[:, CHUNK:A_BAND, :]
    vbuf[:, 0:A_BAND - CHUNK, :] = old_v
    vbuf[:, A_BAND - CHUNK:A_BAND, :] = masked_variants(AB_V0)

    qi = lax.broadcasted_iota(jnp.int32, (2 * CHUNK, A_BAND), 0)
    ki = lax.broadcasted_iota(jnp.int32, (2 * CHUNK, A_BAND), 1)
    first_tile = qi < CHUNK
    dist = jnp.abs(jnp.where(first_tile, qi, qi - CHUNK) + (A_BAND - CHUNK) - ki).astype(F32)
    valid = ki >= (A_BAND - CHUNK) - CHUNK * n
    row_first = lax.broadcasted_iota(jnp.int32, (2 * CHUNK, 1), 0) < CHUNK
    scale = A_HEAD_DIM ** -0.5

    def head_consts(kv, parity):
        ha = 4 * kv + parity
        hb = ha + 2
        slope = jnp.where(row_first, 2.0 ** (-8.0 * (ha + 1) / A_HEADS), 2.0 ** (-8.0 * (hb + 1) / A_HEADS))
        sink = jnp.where(row_first, sinks_ref[ha], sinks_ref[hb])
        return slope * dist, sink

    consts = [[head_consts(kv, par) for par in range(2)] for kv in range(A_KV_HEADS)]

    def attend(b, carry):
        r0 = pl.multiple_of(b * CHUNK, CHUNK)
        for kv in range(A_KV_HEADS):
            c0 = AB_Q0 + 2 * LANES * kv
            q2 = jnp.concatenate([proj_scr[pl.ds(r0, CHUNK), c0:c0 + LANES],
                                  proj_scr[pl.ds(r0, CHUNK), c0 + LANES:c0 + 2 * LANES]], axis=0)
            q2 = (q2 * scale).astype(BF16)
            acc = jnp.zeros((2 * CHUNK, LANES), F32)
            for par in range(2):
                t0 = (2 * kv + par) * LANES
                bias, sink = consts[kv][par]
                s = _dot_nt(q2, kbuf[b, :, t0:t0 + LANES])
                s = jnp.where(valid, s - bias, NEG)
                m = jnp.maximum(jnp.max(s, axis=-1, keepdims=True), sink)
                pr = jnp.exp(s - m)
                den = jnp.sum(pr, axis=-1, keepdims=True) + jnp.exp(sink - m)
                acc = acc + _dot(pr.astype(BF16), vbuf[b, :, t0:t0 + LANES]) / den
            y_scr[pl.ds(r0, CHUNK), c0:c0 + LANES] = acc[0:CHUNK]
            y_scr[pl.ds(r0, CHUNK), c0 + LANES:c0 + 2 * LANES] = acc[CHUNK:2 * CHUNK]
        return carry

    lax.fori_loop(0, bb, attend, 0)

    tail = xpad[:, CHUNK:CHUNK + SUBLANES, :]
    xpad[:, 0:SUBLANES, :] = tail
    xpad[:, SUBLANES:SUBLANES + CHUNK, :] = proj_scr[:, AB_X0:AB_X0 + B_WIDTH].reshape(bb, CHUNK, B_WIDTH)
    conv = convb_ref[...].reshape(1, 1, B_WIDTH)
    for j in range(B_CONV):
        off = SUBLANES - (B_CONV - 1) + j
        conv = conv + convw_ref[j:j + 1, :].reshape(1, 1, B_WIDTH) * xpad[:, off:off + CHUNK, :]
    c2 = conv.reshape(rows, B_WIDTH)
    cb = c2.astype(BF16)
    r = jax.nn.sigmoid(_dot(cb, wa_ref[...]) + ba_ref[...])
    i = jax.nn.sigmoid(_dot(cb, wx_ref[...]) + bx_ref[...])
    log_a = (-RG_C) * r * _softplus(-lam_ref[...])
    a = jnp.exp(log_a).reshape(bb, CHUNK, B_WIDTH)
    u = (jnp.sqrt(_neg_expm1(2.0 * log_a)) * (i * c2)).reshape(bb, CHUNK, B_WIDTH)
    ti = lax.broadcasted_iota(jnp.int32, (bb, CHUNK, B_WIDTH), 1)
    step = 1
    while step < CHUNK:
        keep = ti >= step
        a_prev = jnp.where(keep, pltpu.roll(a, step, 1), 1.0)
        u_prev = jnp.where(keep, pltpu.roll(u, step, 1), 0.0)
        u = u + a * u_prev
        a = a * a_prev
        step *= 2
    h = u + a * hcarry[...].reshape(bb, 1, B_WIDTH)
    hcarry[...] = h[:, CHUNK - 1, :]
    gate = jax.nn.gelu(proj_scr[:, AB_G0:AB_G0 + B_WIDTH], approximate=True)
    y_scr[:, A_WIDTH:A_WIDTH + B_WIDTH] = h.reshape(rows, B_WIDTH) * gate

    y = _dot(y_scr[...].astype(BF16), wout_ref[...])
    out = _layer_norm(DN_ALPHA * x + y, lng_ref[...], lnb_ref[...])
    o_ref[...] = out.reshape(bb, CHUNK, d_model)


def _mixer_ab(x3, win, sinks, convw, convb, wa, ba, wx, bx, lam, wout, lng, lnb):
    b, s, d = x3.shape
    bb = min(MIX_BATCH, b)
    assert b % bb == 0 and s % CHUNK == 0
    rows = bb * CHUNK
    tile = pl.BlockSpec((bb, CHUNK, d), lambda i, j: (i, j, 0))
    in_specs = [pl.BlockSpec(memory_space=pltpu.SMEM), tile, _resident(win.shape),
                _resident(convw.shape), _resident(convb.shape), _resident(wa.shape), _resident(ba.shape),
                _resident(wx.shape), _resident(bx.shape), _resident(lam.shape), _resident(wout.shape),
                _resident(lng.shape), _resident(lnb.shape)]
    return pl.pallas_call(
        functools.partial(_mixer_ab_body, bb=bb),
        out_shape=jax.ShapeDtypeStruct((b, s, d), F32),
        grid=(b // bb, s // CHUNK),
        in_specs=in_specs,
        out_specs=tile,
        scratch_shapes=[
            pltpu.VMEM((rows, AB_COLS), F32),
            pltpu.VMEM((bb, A_BAND, 4 * LANES), BF16),
            pltpu.VMEM((bb, A_BAND, 4 * LANES), BF16),
            pltpu.VMEM((bb, CHUNK + SUBLANES, B_WIDTH), F32),
            pltpu.VMEM((bb, B_WIDTH), F32),
            pltpu.VMEM((rows, A_WIDTH + B_WIDTH), F32),
        ],
        compiler_params=pltpu.CompilerParams(dimension_semantics=("arbitrary", "arbitrary"),
                                             vmem_limit_bytes=VMEM_LIMIT),
        name="mixer_ab",
    )(sinks, x3, win, convw, convb, wa, ba, wx, bx, lam, wout, lng, lnb)


C_GATE_COL = C_HEADS


def _mixer_c_body(x_ref, w1_ref, w2_ref, convw_ref, alog_ref, dtb_ref, ng_ref, wout_ref, lng_ref, lnb_ref,
                  o_ref, xpad, qkv_scr, z_scr, ba_scr, state, og_scr, *, bb):
    n = pl.program_id(1)
    rows = bb * CHUNK
    d_model = x_ref.shape[-1]
    qkv_w = 3 * C_WIDTH

    @pl.when(n == 0)
    def _():
        xpad[...] = jnp.zeros_like(xpad)
        state[...] = jnp.zeros_like(state)

    x = x_ref[...].reshape(rows, d_model)
    xb = x.astype(BF16)
    tail = xpad[:, CHUNK:CHUNK + SUBLANES, :]
    xpad[:, 0:SUBLANES, :] = tail
    xpad[:, SUBLANES:SUBLANES + CHUNK, :] = _dot(xb, w1_ref[:, 0:qkv_w]).reshape(bb, CHUNK, qkv_w)
    z_scr[...] = _dot(xb, w1_ref[:, qkv_w:qkv_w + C_WIDTH])
    ba_scr[...] = _dot(xb, w2_ref[...])

    conv = jnp.zeros((bb, CHUNK, qkv_w), F32)
    for j in range(C_CONV):
        off = SUBLANES - (C_CONV - 1) + j
        conv = conv + convw_ref[j:j + 1, :].reshape(1, 1, qkv_w) * xpad[:, off:off + CHUNK, :]
    qkv_scr[...] = _silu(conv).reshape(rows, qkv_w)

    ri = lax.broadcasted_iota(jnp.int32, (CHUNK, CHUNK), 0)
    ci = lax.broadcasted_iota(jnp.int32, (CHUNK, CHUNK), 1)
    tril = ri >= ci
    strict = ri > ci
    eye = (ri == ci).astype(F32)
    tril_f = tril.astype(F32)
    triu_f = (ri <= ci).astype(F32)
    neg_rate = -jnp.exp(alog_ref[...])
    dtb = dtb_ref[...]
    norm_g = ng_ref[...]
    qscale = C_HEAD_DIM ** -0.5

    def unit(b, carry):
        r0 = pl.multiple_of(b * CHUNK, CHUNK)
        bab = ba_scr[pl.ds(r0, CHUNK), :]
        beta_all = jax.nn.sigmoid(bab)
        g_all = neg_rate * _softplus(bab + dtb)
        gcol = jnp.dot(tril_f, g_all, precision=lax.Precision.HIGHEST, preferred_element_type=F32)
        grow = jnp.dot(g_all.T, triu_f, precision=lax.Precision.HIGHEST, preferred_element_type=F32)
        for h in range(C_HEADS):
            c0 = h * C_HEAD_DIM
            gcl = C_GATE_COL + h
            beta = beta_all[:, h:h + 1]
            gc = gcol[:, gcl:gcl + 1]
            gr = grow[gcl:gcl + 1, :]
            gl = gcol[CHUNK - 1:CHUNK, gcl:gcl + 1]
            q = qkv_scr[pl.ds(r0, CHUNK), c0:c0 + C_HEAD_DIM]
            k = qkv_scr[pl.ds(r0, CHUNK), C_WIDTH + c0:C_WIDTH + c0 + C_HEAD_DIM]
            v = qkv_scr[pl.ds(r0, CHUNK), 2 * C_WIDTH + c0:2 * C_WIDTH + c0 + C_HEAD_DIM]
            q = q * (lax.rsqrt(jnp.sum(q * q, axis=-1, keepdims=True) + NORM_EPS) * qscale)
            k = k * lax.rsqrt(jnp.sum(k * k, axis=-1, keepdims=True) + NORM_EPS)
            kb = k * beta
            eg = jnp.exp(gc)
            kbf = k.astype(BF16)
            qk = _dot_nt(jnp.concatenate([q, kb], axis=0).astype(BF16), kbf)
            decay = jnp.where(tril, jnp.exp(jnp.where(tril, gc - gr, 0.0)), 0.0)
            attn = qk[0:CHUNK] * decay
            lmat = jnp.where(strict, qk[CHUNK:2 * CHUNK] * decay, 0.0)
            tinv = eye - lmat
            pw = lmat.astype(BF16)
            for it in range(5):
                pw_f = _dot(pw, pw)
                pw = pw_f.astype(BF16)
                tinv = tinv + _dot(tinv.astype(BF16), pw)
            rhs = jnp.concatenate([v * beta, kb * eg], axis=1).astype(BF16)
            uw = _dot(tinv.astype(BF16), rhs)
            st = state[b * C_HEADS + h]
            wq = _dot(jnp.concatenate([uw[:, C_HEAD_DIM:], q * eg], axis=0).astype(BF16), st.astype(BF16))
            v_new = uw[:, 0:C_HEAD_DIM] - wq[0:CHUNK]
            vnb = v_new.astype(BF16)
            o = wq[CHUNK:2 * CHUNK] + _dot(attn.astype(BF16), vnb)
            kdec = (k * jnp.exp(gl - gc)).astype(BF16)
            state[b * C_HEADS + h] = st * jnp.exp(gl) + _dot_tn(kdec, vnb)
            o = o * lax.rsqrt(jnp.mean(o * o, axis=-1, keepdims=True) + NORM_EPS) * norm_g
            og_scr[pl.ds(r0, CHUNK), c0:c0 + C_HEAD_DIM] = o * _silu(z_scr[pl.ds(r0, CHUNK), c0:c0 + C_HEAD_DIM])
        return carry

    lax.fori_loop(0, bb, unit, 0)

    y = _dot(og_scr[...].astype(BF16), wout_ref[...])
    out = _layer_norm(DN_ALPHA * x + y, lng_ref[...], lnb_ref[...])
    o_ref[...] = out.reshape(bb, CHUNK, d_model)


def _mixer_c(x3, w1, w2, convw, alog, dtb, ng, wout, lng, lnb):
    b, s, d = x3.shape
    bb = min(MIX_BATCH, b)
    assert b % bb == 0 and s % CHUNK == 0
    rows = bb * CHUNK
    tile = pl.BlockSpec((bb, CHUNK, d), lambda i, j: (i, j, 0))
    in_specs = [tile] + [_resident(a.shape) for a in (w1, w2, convw, alog, dtb, ng, wout, lng, lnb)]
    return pl.pallas_call(
        functools.partial(_mixer_c_body, bb=bb),
        out_shape=jax.ShapeDtypeStruct((b, s, d), F32),
        grid=(b // bb, s // CHUNK),
        in_specs=in_specs,
        out_specs=tile,
        scratch_shapes=[
            pltpu.VMEM((bb, CHUNK + SUBLANES, 3 * C_WIDTH), F32),
            pltpu.VMEM((rows, 3 * C_WIDTH), F32),
            pltpu.VMEM((rows, C_WIDTH), F32),
            pltpu.VMEM((rows, LANES), F32),
            pltpu.VMEM((bb * C_HEADS, C_HEAD_DIM, C_HEAD_DIM), F32),
            pltpu.VMEM((rows, C_WIDTH), F32),
        ],
        compiler_params=pltpu.CompilerParams(dimension_semantics=("arbitrary", "arbitrary"),
                                             vmem_limit_bytes=VMEM_LIMIT),
        name="mixer_c",
    )(x3, w1, w2, convw, alog, dtb, ng, wout, lng, lnb)


def _block_diag(w):
    nb, bs, _ = w.shape
    out = jnp.zeros((nb * bs, nb * bs), w.dtype)
    for i in range(nb):
        out = out.at[i * bs:(i + 1) * bs, i * bs:(i + 1) * bs].set(w[i])
    return out


def _row(v):
    return v.reshape(1, -1).astype(F32)


def kernel(x, p, ffn1_wg, ffn1_wu, ffn1_wd, ffn2_wg, ffn2_wu, ffn2_wd, ln_g, ln_b, ple_wg, ple_bg, ple_wp, ab_w_in, a_sinks, b_conv_w, b_conv_b, b_wa, b_ba, b_wx, b_bx, b_lam, ab_w_out, c_w_in, c_conv_w, c_a_log, c_dt_bias, c_norm_g, c_w_out):
    b, s, d = x.shape
    n = b * s
    for i in range(DEPTH):
        j = i // 2
        x = _ffn(x.reshape(n, d), ffn1_wg[i].astype(BF16), ffn1_wu[i].astype(BF16), ffn1_wd[i].astype(BF16),
                 _row(ln_g[i, 0]), _row(ln_b[i, 0])).reshape(b, s, d)
        if i % 2 == 0:
            w = ab_w_in[j]
            o1, o2, o3 = A_WIDTH, A_WIDTH + A_KV_WIDTH, A_WIDTH + 2 * A_KV_WIDTH
            hd = A_HEAD_DIM
            swap = lambda m: jnp.concatenate([m[:, hd:2 * hd], m[:, 0:hd]], axis=1)
            win = jnp.concatenate([w[:, :o1], w[:, o1:o2], swap(w[:, o1:o2]), w[:, o2:o3], swap(w[:, o2:o3]),
                                   w[:, o3:]], axis=1).astype(BF16)
            x = _mixer_ab(x, win, a_sinks[j].astype(F32), b_conv_w[j], _row(b_conv_b[j]),
                          _block_diag(b_wa[j]).astype(BF16), _row(b_ba[j]),
                          _block_diag(b_wx[j]).astype(BF16), _row(b_bx[j]), _row(b_lam[j]),
                          ab_w_out[j].astype(BF16), _row(ln_g[i, 1]), _row(ln_b[i, 1]))
        else:
            w = c_w_in[j]
            w1 = w[:, :4 * C_WIDTH].astype(BF16)
            w2 = jnp.pad(w[:, 4 * C_WIDTH:], ((0, 0), (0, LANES - 2 * C_HEADS))).astype(BF16)
            pad_gate = lambda v: jnp.pad(v.astype(F32), (C_GATE_COL, LANES - C_GATE_COL - C_HEADS)).reshape(1, LANES)
            x = _mixer_c(x, w1, w2, c_conv_w[j], pad_gate(c_a_log[j]), pad_gate(c_dt_bias[j]),
                         _row(c_norm_g[j]), c_w_out[j].astype(BF16), _row(ln_g[i, 1]), _row(ln_b[i, 1]))
        x = _ffn(x.reshape(n, d), ffn2_wg[i].astype(BF16), ffn2_wu[i].astype(BF16), ffn2_wd[i].astype(BF16),
                 _row(ln_g[i, 2]), _row(ln_b[i, 2]),
                 ple_args=(p[i].reshape(n, -1), ple_wg[i].astype(BF16), _row(ple_bg[i]),
                           ple_wp[i].astype(BF16))).reshape(b, s, d)
    return x
```

```python
import functools

import jax
import jax.numpy as jnp
from jax import lax
from jax.experimental import pallas as pl
from jax.experimental.pallas import tpu as pltpu

F32 = jnp.float32
BF16 = jnp.bfloat16

DEPTH = 2
CHUNK = 64
A_HEADS = 8
A_KV_HEADS = 2
A_HEAD_DIM = 64
A_WIDTH = A_HEADS * A_HEAD_DIM
A_KV_WIDTH = A_KV_HEADS * A_HEAD_DIM
A_PREV_CHUNKS = 2
A_BAND = (A_PREV_CHUNKS + 1) * CHUNK
B_WIDTH = 512
B_BLOCKS = 8
B_CONV = 4
RG_C = 8.0
C_HEADS = 8
C_HEAD_DIM = 128
C_WIDTH = C_HEADS * C_HEAD_DIM
C_CONV = 4
DN_ALPHA = (2.0 * DEPTH) ** 0.25
LN_EPS = 1e-5
NORM_EPS = 1e-6
NEG = -1e30

LANES = 128
SUBLANES = 8
FFN_ROWS = 512
FFN_COLS = 256
MIX_BATCH = 8
VMEM_LIMIT = 56 * 1024 * 1024


def _dot(a, b):
    return jnp.dot(a, b, preferred_element_type=F32)


def _dot_nt(a, b):
    return lax.dot_general(a, b, (((1,), (1,)), ((), ())), preferred_element_type=F32)


def _dot_tn(a, b):
    return lax.dot_general(a, b, (((0,), (0,)), ((), ())), preferred_element_type=F32)


def _layer_norm(z, g, b):
    mu = jnp.mean(z, axis=-1, keepdims=True)
    d = z - mu
    var = jnp.mean(d * d, axis=-1, keepdims=True)
    return d * lax.rsqrt(var + LN_EPS) * g + b


def _silu(x):
    return x * jax.nn.sigmoid(x)


def _softplus(x):
    e = jnp.exp(-jnp.abs(x))
    u = 1.0 + e
    log1p_e = jnp.where(u == 1.0, e, jnp.log(u) * (e / (u - 1.0)))
    return jnp.maximum(x, 0.0) + log1p_e


def _neg_expm1(y):
    return -jnp.tanh(0.5 * y) * (jnp.exp(y) + 1.0)


def _resident(shape):
    zeros = (0,) * len(shape)
    return pl.BlockSpec(shape, lambda *_: zeros, pipeline_mode=pl.Buffered(1))


def _ffn_body(x_ref, wg_ref, wu_ref, wd_ref, lng_ref, lnb_ref, *rest, ple, n_chunks):
    if ple:
        p_ref, pwg_ref, pbg_ref, pwp_ref, o_ref, h_scr = rest
    else:
        o_ref, h_scr = rest
    x = x_ref[...]
    xb = x.astype(BF16)
    for c in range(n_chunks):
        sl = slice(c * FFN_COLS, (c + 1) * FFN_COLS)
        g = _dot(xb, wg_ref[:, sl])
        u = _dot(xb, wu_ref[:, sl])
        h_scr[:, sl] = (_silu(g) * u).astype(BF16)
    y = _dot(h_scr[...], wd_ref[...])
    out = _layer_norm(DN_ALPHA * x + 0.5 * y, lng_ref[...], lnb_ref[...])
    if ple:
        gate = jax.nn.sigmoid(_dot(out.astype(BF16), pwg_ref[...]) + pbg_ref[...])
        out = out + gate * _dot(p_ref[...].astype(BF16), pwp_ref[...])
    o_ref[...] = out


def _ffn(x2, wg, wu, wd, lng, lnb, ple_args=None):
    n, d = x2.shape
    f = wg.shape[1]
    rows = min(FFN_ROWS, n)
    assert n % rows == 0 and f % FFN_COLS == 0
    row_spec = pl.BlockSpec((rows, d), lambda i: (i, 0))
    in_specs = [row_spec, _resident((d, f)), _resident((d, f)), _resident((f, d)),
                _resident((1, d)), _resident((1, d))]
    args = [x2, wg, wu, wd, lng, lnb]
    if ple_args is not None:
        p2, pwg, pbg, pwp = ple_args
        dp = p2.shape[1]
        in_specs += [pl.BlockSpec((rows, dp), lambda i: (i, 0)), _resident((d, d)),
                     _resident((1, d)), _resident((dp, d))]
        args += [p2, pwg, pbg, pwp]
    return pl.pallas_call(
        functools.partial(_ffn_body, ple=ple_args is not None, n_chunks=f // FFN_COLS),
        out_shape=jax.ShapeDtypeStruct((n, d), F32),
        grid=(n // rows,),
        in_specs=in_specs,
        out_specs=row_spec,
        scratch_shapes=[pltpu.VMEM((rows, f), BF16)],
        compiler_params=pltpu.CompilerParams(dimension_semantics=("arbitrary",),
                                             vmem_limit_bytes=VMEM_LIMIT),
        name="ffn_ple" if ple_args is not None else "ffn",
    )(*args)


AB_Q0 = 0
AB_K0 = A_WIDTH
AB_V0 = AB_K0 + 2 * A_KV_WIDTH
AB_X0 = AB_V0 + 2 * A_KV_WIDTH
AB_G0 = AB_X0 + B_WIDTH
AB_COLS = AB_G0 + B_WIDTH


def _mixer_ab_body(sinks_ref, x_ref, win_ref, convw_ref, convb_ref, wa_ref, ba_ref, wx_ref, bx_ref,
                   lam_ref, wout_ref, lng_ref, lnb_ref, o_ref,
                   proj_scr, kbuf, vbuf, xpad, hcarry, y_scr, *, bb):
    n = pl.program_id(1)
    rows = bb * CHUNK
    d_model = x_ref.shape[-1]

    @pl.when(n == 0)
    def _():
        kbuf[...] = jnp.zeros_like(kbuf)
        vbuf[...] = jnp.zeros_like(vbuf)
        xpad[...] = jnp.zeros_like(xpad)
        hcarry[...] = jnp.zeros_like(hcarry)

    x = x_ref[...].reshape(rows, d_model)
    proj_scr[...] = _dot(x.astype(BF16), win_ref[...])

    lo = lax.broadcasted_iota(jnp.int32, (rows, LANES), 1) < A_HEAD_DIM

    def masked_variants(c0):
        ta = proj_scr[:, c0:c0 + LANES]
        tb = proj_scr[:, c0 + LANES:c0 + 2 * LANES]
        zero = jnp.zeros_like(ta)
        parts = [jnp.where(lo, ta, zero), jnp.where(lo, zero, tb),
                 jnp.where(lo, tb, zero), jnp.where(lo, zero, ta)]
        return jnp.concatenate(parts, axis=1).astype(BF16).reshape(bb, CHUNK, 4 * LANES)

    old_k = kbuf[:, CHUNK:A_BAND, :]
    kbuf[:, 0:A_BAND - CHUNK, :] = old_k
    kbuf[:, A_BAND - CHUNK:A_BAND, :] = masked_variants(AB_K0)
    old_v = vbuf[:, CHUNK:A_BAND, :]
    vbuf[:, 0:A_BAND - CHUNK, :] = old_v
    vbuf[:, A_BAND - CHUNK:A_BAND, :] = masked_variants(AB_V0)

    qi = lax.broadcasted_iota(jnp.int32, (2 * CHUNK, A_BAND), 0)
    ki = lax.broadcasted_iota(jnp.int32, (2 * CHUNK, A_BAND), 1)
    first_tile = qi < CHUNK
    dist = jnp.abs(jnp.where(first_tile, qi, qi - CHUNK) + (A_BAND - CHUNK) - ki).astype(F32)
    valid = ki >= (A_BAND - CHUNK) - CHUNK * n
    row_first = lax.broadcasted_iota(jnp.int32, (2 * CHUNK, 1), 0) < CHUNK
    scale = A_HEAD_DIM ** -0.5

    def head_consts(kv, parity):
        ha = 4 * kv + parity
        hb = ha + 2
        slope = jnp.where(row_first, 2.0 ** (-8.0 * (ha + 1) / A_HEADS), 2.0 ** (-8.0 * (hb + 1) / A_HEADS))
        sink = jnp.where(row_first, sinks_ref[ha], sinks_ref[hb])
        return slope * dist, sink

    consts = [[head_consts(kv, par) for par in range(2)] for kv in range(A_KV_HEADS)]

    def attend(b, carry):
        r0 = pl.multiple_of(b * CHUNK, CHUNK)
        for kv in range(A_KV_HEADS):
            c0 = AB_Q0 + 2 * LANES * kv
            q2 = jnp.concatenate([proj_scr[pl.ds(r0, CHUNK), c0:c0 + LANES],
                                  proj_scr[pl.ds(r0, CHUNK), c0 + LANES:c0 + 2 * LANES]], axis=0)
            q2 = (q2 * scale).astype(BF16)
            acc = jnp.zeros((2 * CHUNK, LANES), F32)
            for par in range(2):
                t0 = (2 * kv + par) * LANES
                bias, sink = consts[kv][par]
                s = _dot_nt(q2, kbuf[b, :, t0:t0 + LANES])
                s = jnp.where(valid, s - bias, NEG)
                m = jnp.maximum(jnp.max(s, axis=-1, keepdims=True), sink)
                pr = jnp.exp(s - m)
                den = jnp.sum(pr, axis=-1, keepdims=True) + jnp.exp(sink - m)
                acc = acc + _dot(pr.astype(BF16), vbuf[b, :, t0:t0 + LANES]) / den
            y_scr[pl.ds(r0, CHUNK), c0:c0 + LANES] = acc[0:CHUNK]
            y_scr[pl.ds(r0, CHUNK), c0 + LANES:c0 + 2 * LANES] = acc[CHUNK:2 * CHUNK]
        return carry

    lax.fori_loop(0, bb, attend, 0)

    tail = xpad[:, CHUNK:CHUNK + SUBLANES, :]
    xpad[:, 0:SUBLANES, :] = tail
    xpad[:, SUBLANES:SUBLANES + CHUNK, :] = proj_scr[:, AB_X0:AB_X0 + B_WIDTH].reshape(bb, CHUNK, B_WIDTH)
    conv = convb_ref[...].reshape(1, 1, B_WIDTH)
    for j in range(B_CONV):
        off = SUBLANES - (B_CONV - 1) + j
        conv = conv + convw_ref[j:j + 1, :].reshape(1, 1, B_WIDTH) * xpad[:, off:off + CHUNK, :]
    c2 = conv.reshape(rows, B_WIDTH)
    cb = c2.astype(BF16)
    r = jax.nn.sigmoid(_dot(cb, wa_ref[...]) + ba_ref[...])
    i = jax.nn.sigmoid(_dot(cb, wx_ref[...]) + bx_ref[...])
    log_a = (-RG_C) * r * _softplus(-lam_ref[...])
    a = jnp.exp(log_a).reshape(bb, CHUNK, B_WIDTH)
    u = (jnp.sqrt(_neg_expm1(2.0 * log_a)) * (i * c2)).reshape(bb, CHUNK, B_WIDTH)
    ti = lax.broadcasted_iota(jnp.int32, (bb, CHUNK, B_WIDTH), 1)
    step = 1
    while step < CHUNK:
        keep = ti >= step
        a_prev = jnp.where(keep, pltpu.roll(a, step, 1), 1.0)
        u_prev = jnp.where(keep, pltpu.roll(u, step, 1), 0.0)
        u = u + a * u_prev
        a = a * a_prev
        step *= 2
    h = u + a * hcarry[...].reshape(bb, 1, B_WIDTH)
    hcarry[...] = h[:, CHUNK - 1, :]
    gate = jax.nn.gelu(proj_scr[:, AB_G0:AB_G0 + B_WIDTH], approximate=True)
    y_scr[:, A_WIDTH:A_WIDTH + B_WIDTH] = h.reshape(rows, B_WIDTH) * gate

    y = _dot(y_scr[...].astype(BF16), wout_ref[...])
    out = _layer_norm(DN_ALPHA * x + y, lng_ref[...], lnb_ref[...])
    o_ref[...] = out.reshape(bb, CHUNK, d_model)


def _mixer_ab(x3, win, sinks, convw, convb, wa, ba, wx, bx, lam, wout, lng, lnb):
    b, s, d = x3.shape
    bb = min(MIX_BATCH, b)
    assert b % bb == 0 and s % CHUNK == 0
    rows = bb * CHUNK
    tile = pl.BlockSpec((bb, CHUNK, d), lambda i, j: (i, j, 0))
    in_specs = [pl.BlockSpec(memory_space=pltpu.SMEM), tile, _resident(win.shape),
                _resident(convw.shape), _resident(convb.shape), _resident(wa.shape), _resident(ba.shape),
                _resident(wx.shape), _resident(bx.shape), _resident(lam.shape), _resident(wout.shape),
                _resident(lng.shape), _resident(lnb.shape)]
    return pl.pallas_call(
        functools.partial(_mixer_ab_body, bb=bb),
        out_shape=jax.ShapeDtypeStruct((b, s, d), F32),
        grid=(b // bb, s // CHUNK),
        in_specs=in_specs,
        out_specs=tile,
        scratch_shapes=[
            pltpu.VMEM((rows, AB_COLS), F32),
            pltpu.VMEM((bb, A_BAND, 4 * LANES), BF16),
            pltpu.VMEM((bb, A_BAND, 4 * LANES), BF16),
            pltpu.VMEM((bb, CHUNK + SUBLANES, B_WIDTH), F32),
            pltpu.VMEM((bb, B_WIDTH), F32),
            pltpu.VMEM((rows, A_WIDTH + B_WIDTH), F32),
        ],
        compiler_params=pltpu.CompilerParams(dimension_semantics=("arbitrary", "arbitrary"),
                                             vmem_limit_bytes=VMEM_LIMIT),
        name="mixer_ab",
    )(sinks, x3, win, convw, convb, wa, ba, wx, bx, lam, wout, lng, lnb)


C_GATE_COL = C_HEADS


def _mixer_c_body(x_ref, w1_ref, w2_ref, convw_ref, alog_ref, dtb_ref, ng_ref, wout_ref, lng_ref, lnb_ref,
                  o_ref, xpad, qkv_scr, z_scr, ba_scr, state, og_scr, *, bb):
    n = pl.program_id(1)
    rows = bb * CHUNK
    d_model = x_ref.shape[-1]
    qkv_w = 3 * C_WIDTH

    @pl.when(n == 0)
    def _():
        xpad[...] = jnp.zeros_like(xpad)
        state[...] = jnp.zeros_like(state)

    x = x_ref[...].reshape(rows, d_model)
    xb = x.astype(BF16)
    tail = xpad[:, CHUNK:CHUNK + SUBLANES, :]
    xpad[:, 0:SUBLANES, :] = tail
    xpad[:, SUBLANES:SUBLANES + CHUNK, :] = _dot(xb, w1_ref[:, 0:qkv_w]).reshape(bb, CHUNK, qkv_w)
    z_scr[...] = _dot(xb, w1_ref[:, qkv_w:qkv_w + C_WIDTH])
    ba_scr[...] = _dot(xb, w2_ref[...])

    conv = jnp.zeros((bb, CHUNK, qkv_w), F32)
    for j in range(C_CONV):
        off = SUBLANES - (C_CONV - 1) + j
        conv = conv + convw_ref[j:j + 1, :].reshape(1, 1, qkv_w) * xpad[:, off:off + CHUNK, :]
    qkv_scr[...] = _silu(conv).reshape(rows, qkv_w)

    ri = lax.broadcasted_iota(jnp.int32, (CHUNK, CHUNK), 0)
    ci = lax.broadcasted_iota(jnp.int32, (CHUNK, CHUNK), 1)
    tril = ri >= ci
    strict = ri > ci
    eye = (ri == ci).astype(F32)
    tril_f = tril.astype(F32)
    triu_f = (ri <= ci).astype(F32)
    neg_rate = -jnp.exp(alog_ref[...])
    dtb = dtb_ref[...]
    norm_g = ng_ref[...]
    qscale = C_HEAD_DIM ** -0.5

    def unit(b, carry):
        heads = range(C_HEADS)
        r0 = pl.multiple_of(b * CHUNK, CHUNK)
        rsl = pl.ds(r0, CHUNK)
        st = [state[b * C_HEADS + h] for h in heads]
        bab = ba_scr[rsl, :]
        beta_all = jax.nn.sigmoid(bab)
        g_all = neg_rate * _softplus(bab + dtb)
        gcol = jnp.dot(tril_f, g_all, precision=lax.Precision.HIGHEST, preferred_element_type=F32)
        grow = jnp.dot(g_all.T, triu_f, precision=lax.Precision.HIGHEST, preferred_element_type=F32)
        beta = [beta_all[:, h:h + 1] for h in heads]
        gc = [gcol[:, C_GATE_COL + h:C_GATE_COL + h + 1] for h in heads]
        gr = [grow[C_GATE_COL + h:C_GATE_COL + h + 1, :] for h in heads]
        gl = [gcol[CHUNK - 1:CHUNK, C_GATE_COL + h:C_GATE_COL + h + 1] for h in heads]

        def unit_norm(t, extra):
            return t * (lax.rsqrt(jnp.sum(t * t, axis=-1, keepdims=True) + NORM_EPS) * extra)

        def head_cols(part, h):
            return slice(part * C_WIDTH + h * C_HEAD_DIM, part * C_WIDTH + (h + 1) * C_HEAD_DIM)

        q = [unit_norm(qkv_scr[rsl, head_cols(0, h)], qscale) for h in heads]
        k = [unit_norm(qkv_scr[rsl, head_cols(1, h)], 1.0) for h in heads]
        v = [qkv_scr[rsl, head_cols(2, h)] for h in heads]
        kb = [k[h] * beta[h] for h in heads]
        eg = [jnp.exp(gc[h]) for h in heads]
        qk = [_dot_nt(jnp.concatenate([q[h], kb[h]], axis=0).astype(BF16), k[h].astype(BF16)) for h in heads]
        decay = [jnp.where(tril, jnp.exp(jnp.where(tril, gc[h] - gr[h], 0.0)), 0.0) for h in heads]
        attn = [(qk[h][0:CHUNK] * decay[h]).astype(BF16) for h in heads]
        lmat = [jnp.where(strict, qk[h][CHUNK:2 * CHUNK] * decay[h], 0.0) for h in heads]
        tinv = [eye - lmat[h] for h in heads]
        pw = [lmat[h].astype(BF16) for h in heads]
        for _ in range(5):
            pw = [_dot(pw[h], pw[h]).astype(BF16) for h in heads]
            tinv = [tinv[h] + _dot(tinv[h].astype(BF16), pw[h]) for h in heads]
        rhs = [jnp.concatenate([v[h] * beta[h], kb[h] * eg[h]], axis=1).astype(BF16) for h in heads]
        uw = [_dot(tinv[h].astype(BF16), rhs[h]) for h in heads]
        wq = [_dot(jnp.concatenate([uw[h][:, C_HEAD_DIM:], q[h] * eg[h]], axis=0).astype(BF16), st[h].astype(BF16))
              for h in heads]
        vnb = [(uw[h][:, 0:C_HEAD_DIM] - wq[h][0:CHUNK]).astype(BF16) for h in heads]
        o = [wq[h][CHUNK:2 * CHUNK] + _dot(attn[h], vnb[h]) for h in heads]
        kdec = [(k[h] * jnp.exp(gl[h] - gc[h])).astype(BF16) for h in heads]
        new_st = [st[h] * jnp.exp(gl[h]) + _dot_tn(kdec[h], vnb[h]) for h in heads]
        for h in heads:
            state[b * C_HEADS + h] = new_st[h]
            on = o[h] * lax.rsqrt(jnp.mean(o[h] * o[h], axis=-1, keepdims=True) + NORM_EPS) * norm_g
            og_scr[rsl, head_cols(0, h)] = on * _silu(z_scr[rsl, head_cols(0, h)])
        return carry

    lax.fori_loop(0, bb, unit, 0)

    y = _dot(og_scr[...].astype(BF16), wout_ref[...])
    out = _layer_norm(DN_ALPHA * x + y, lng_ref[...], lnb_ref[...])
    o_ref[...] = out.reshape(bb, CHUNK, d_model)


def _mixer_c(x3, w1, w2, convw, alog, dtb, ng, wout, lng, lnb):
    b, s, d = x3.shape
    bb = min(MIX_BATCH, b)
    assert b % bb == 0 and s % CHUNK == 0
    rows = bb * CHUNK
    tile = pl.BlockSpec((bb, CHUNK, d), lambda i, j: (i, j, 0))
    in_specs = [tile] + [_resident(a.shape) for a in (w1, w2, convw, alog, dtb, ng, wout, lng, lnb)]
    return pl.pallas_call(
        functools.partial(_mixer_c_body, bb=bb),
        out_shape=jax.ShapeDtypeStruct((b, s, d), F32),
        grid=(b // bb, s // CHUNK),
        in_specs=in_specs,
        out_specs=tile,
        scratch_shapes=[
            pltpu.VMEM((bb, CHUNK + SUBLANES, 3 * C_WIDTH), F32),
            pltpu.VMEM((rows, 3 * C_WIDTH), F32),
            pltpu.VMEM((rows, C_WIDTH), F32),
            pltpu.VMEM((rows, LANES), F32),
            pltpu.VMEM((bb * C_HEADS, C_HEAD_DIM, C_HEAD_DIM), F32),
            pltpu.VMEM((rows, C_WIDTH), F32),
        ],
        compiler_params=pltpu.CompilerParams(dimension_semantics=("arbitrary", "arbitrary"),
                                             vmem_limit_bytes=VMEM_LIMIT),
        name="mixer_c",
    )(x3, w1, w2, convw, alog, dtb, ng, wout, lng, lnb)


def _block_diag(w):
    nb, bs, _ = w.shape
    out = jnp.zeros((nb * bs, nb * bs), w.dtype)
    for i in range(nb):
        out = out.at[i * bs:(i + 1) * bs, i * bs:(i + 1) * bs].set(w[i])
    return out


def _row(v):
    return v.reshape(1, -1).astype(F32)


def kernel(x, p, ffn1_wg, ffn1_wu, ffn1_wd, ffn2_wg, ffn2_wu, ffn2_wd, ln_g, ln_b, ple_wg, ple_bg, ple_wp, ab_w_in, a_sinks, b_conv_w, b_conv_b, b_wa, b_ba, b_wx, b_bx, b_lam, ab_w_out, c_w_in, c_conv_w, c_a_log, c_dt_bias, c_norm_g, c_w_out):
    b, s, d = x.shape
    n = b * s
    for i in range(DEPTH):
        j = i // 2
        x = _ffn(x.reshape(n, d), ffn1_wg[i].astype(BF16), ffn1_wu[i].astype(BF16), ffn1_wd[i].astype(BF16),
                 _row(ln_g[i, 0]), _row(ln_b[i, 0])).reshape(b, s, d)
        if i % 2 == 0:
            w = ab_w_in[j]
            o1, o2, o3 = A_WIDTH, A_WIDTH + A_KV_WIDTH, A_WIDTH + 2 * A_KV_WIDTH
            hd = A_HEAD_DIM
            swap = lambda m: jnp.concatenate([m[:, hd:2 * hd], m[:, 0:hd]], axis=1)
            win = jnp.concatenate([w[:, :o1], w[:, o1:o2], swap(w[:, o1:o2]), w[:, o2:o3], swap(w[:, o2:o3]),
                                   w[:, o3:]], axis=1).astype(BF16)
            x = _mixer_ab(x, win, a_sinks[j].astype(F32), b_conv_w[j], _row(b_conv_b[j]),
                          _block_diag(b_wa[j]).astype(BF16), _row(b_ba[j]),
                          _block_diag(b_wx[j]).astype(BF16), _row(b_bx[j]), _row(b_lam[j]),
                          ab_w_out[j].astype(BF16), _row(ln_g[i, 1]), _row(ln_b[i, 1]))
        else:
            w = c_w_in[j]
            w1 = w[:, :4 * C_WIDTH].astype(BF16)
            w2 = jnp.pad(w[:, 4 * C_WIDTH:], ((0, 0), (0, LANES - 2 * C_HEADS))).astype(BF16)
            pad_gate = lambda v: jnp.pad(v.astype(F32), (C_GATE_COL, LANES - C_GATE_COL - C_HEADS)).reshape(1, LANES)
            x = _mixer_c(x, w1, w2, c_conv_w[j], pad_gate(c_a_log[j]), pad_gate(c_dt_bias[j]),
                         _row(c_norm_g[j]), c_w_out[j].astype(BF16), _row(ln_g[i, 1]), _row(ln_b[i, 1]))
        x = _ffn(x.reshape(n, d), ffn2_wg[i].astype(BF16), ffn2_wu[i].astype(BF16), ffn2_wd[i].astype(BF16),
                 _row(ln_g[i, 2]), _row(ln_b[i, 2]),
                 ple_args=(p[i].reshape(n, -1), ple_wg[i].astype(BF16), _row(ple_bg[i]),
                           ple_wp[i].astype(BF16))).reshape(b, s, d)
    return x
```

```python
import functools

import jax
import jax.numpy as jnp
from jax import lax
from jax.experimental import pallas as pl
from jax.experimental.pallas import tpu as pltpu

F32 = jnp.float32
BF16 = jnp.bfloat16

DEPTH = 2
CHUNK = 64
A_HEADS = 8
A_KV_HEADS = 2
A_HEAD_DIM = 64
A_WIDTH = A_HEADS * A_HEAD_DIM
A_KV_WIDTH = A_KV_HEADS * A_HEAD_DIM
A_PREV_CHUNKS = 2
A_BAND = (A_PREV_CHUNKS + 1) * CHUNK
B_WIDTH = 512
B_BLOCKS = 8
B_CONV = 4
RG_C = 8.0
C_HEADS = 8
C_HEAD_DIM = 128
C_WIDTH = C_HEADS * C_HEAD_DIM
C_CONV = 4
DN_ALPHA = (2.0 * DEPTH) ** 0.25
LN_EPS = 1e-5
NORM_EPS = 1e-6
NEG = -1e30

LANES = 128
SUBLANES = 8
FFN_ROWS = 512
FFN_COLS = 256
MIX_BATCH = 8
C_GROUP = 4
VMEM_LIMIT = 56 * 1024 * 1024


def _dot(a, b):
    return jnp.dot(a, b, preferred_element_type=F32)


def _dot_nt(a, b):
    return lax.dot_general(a, b, (((1,), (1,)), ((), ())), preferred_element_type=F32)


def _dot_tn(a, b):
    return lax.dot_general(a, b, (((0,), (0,)), ((), ())), preferred_element_type=F32)


def _layer_norm(z, g, b):
    mu = jnp.mean(z, axis=-1, keepdims=True)
    d = z - mu
    var = jnp.mean(d * d, axis=-1, keepdims=True)
    return d * lax.rsqrt(var + LN_EPS) * g + b


def _silu(x):
    return x * jax.nn.sigmoid(x)


def _softplus(x):
    e = jnp.exp(-jnp.abs(x))
    u = 1.0 + e
    log1p_e = jnp.where(u == 1.0, e, jnp.log(u) * (e / (u - 1.0)))
    return jnp.maximum(x, 0.0) + log1p_e


def _neg_expm1(y):
    return -jnp.tanh(0.5 * y) * (jnp.exp(y) + 1.0)


def _resident(shape):
    zeros = (0,) * len(shape)
    return pl.BlockSpec(shape, lambda *_: zeros, pipeline_mode=pl.Buffered(1))


def _ffn_body(x_ref, wg_ref, wu_ref, wd_ref, lng_ref, lnb_ref, *rest, ple, n_chunks):
    if ple:
        p_ref, pwg_ref, pbg_ref, pwp_ref, o_ref, h_scr = rest
    else:
        o_ref, h_scr = rest
    x = x_ref[...]
    xb = x.astype(BF16)
    for c in range(n_chunks):
        sl = slice(c * FFN_COLS, (c + 1) * FFN_COLS)
        g = _dot(xb, wg_ref[:, sl])
        u = _dot(xb, wu_ref[:, sl])
        h_scr[:, sl] = (_silu(g) * u).astype(BF16)
    y = _dot(h_scr[...], wd_ref[...])
    out = _layer_norm(DN_ALPHA * x + 0.5 * y, lng_ref[...], lnb_ref[...])
    if ple:
        gate = jax.nn.sigmoid(_dot(out.astype(BF16), pwg_ref[...]) + pbg_ref[...])
        out = out + gate * _dot(p_ref[...].astype(BF16), pwp_ref[...])
    o_ref[...] = out


def _ffn(x2, wg, wu, wd, lng, lnb, ple_args=None):
    n, d = x2.shape
    f = wg.shape[1]
    rows = min(FFN_ROWS, n)
    assert n % rows == 0 and f % FFN_COLS == 0
    row_spec = pl.BlockSpec((rows, d), lambda i: (i, 0))
    in_specs = [row_spec, _resident((d, f)), _resident((d, f)), _resident((f, d)),
                _resident((1, d)), _resident((1, d))]
    args = [x2, wg, wu, wd, lng, lnb]
    if ple_args is not None:
        p2, pwg, pbg, pwp = ple_args
        dp = p2.shape[1]
        in_specs += [pl.BlockSpec((rows, dp), lambda i: (i, 0)), _resident((d, d)),
                     _resident((1, d)), _resident((dp, d))]
        args += [p2, pwg, pbg, pwp]
    return pl.pallas_call(
        functools.partial(_ffn_body, ple=ple_args is not None, n_chunks=f // FFN_COLS),
        out_shape=jax.ShapeDtypeStruct((n, d), F32),
        grid=(n // rows,),
        in_specs=in_specs,
        out_specs=row_spec,
        scratch_shapes=[pltpu.VMEM((rows, f), BF16)],
        compiler_params=pltpu.CompilerParams(dimension_semantics=("arbitrary",),
                                             vmem_limit_bytes=VMEM_LIMIT),
        name="ffn_ple" if ple_args is not None else "ffn",
    )(*args)


AB_Q0 = 0
AB_K0 = A_WIDTH
AB_V0 = AB_K0 + 2 * A_KV_WIDTH
AB_X0 = AB_V0 + 2 * A_KV_WIDTH
AB_G0 = AB_X0 + B_WIDTH
AB_COLS = AB_G0 + B_WIDTH


def _mixer_ab_body(sinks_ref, x_ref, win_ref, convw_ref, convb_ref, wa_ref, ba_ref, wx_ref, bx_ref,
                   lam_ref, wout_ref, lng_ref, lnb_ref, o_ref,
                   proj_scr, kbuf, vbuf, xpad, hcarry, y_scr, *, bb):
    n = pl.program_id(1)
    rows = bb * CHUNK
    d_model = x_ref.shape[-1]

    @pl.when(n == 0)
    def _():
        kbuf[...] = jnp.zeros_like(kbuf)
        vbuf[...] = jnp.zeros_like(vbuf)
        xpad[...] = jnp.zeros_like(xpad)
        hcarry[...] = jnp.zeros_like(hcarry)

    x = x_ref[...].reshape(rows, d_model)
    proj_scr[...] = _dot(x.astype(BF16), win_ref[...])

    lo = lax.broadcasted_iota(jnp.int32, (rows, LANES), 1) < A_HEAD_DIM

    def masked_variants(c0):
        ta = proj_scr[:, c0:c0 + LANES]
        tb = proj_scr[:, c0 + LANES:c0 + 2 * LANES]
        zero = jnp.zeros_like(ta)
        parts = [jnp.where(lo, ta, zero), jnp.where(lo, zero, tb),
                 jnp.where(lo, tb, zero), jnp.where(lo, zero, ta)]
        return jnp.concatenate(parts, axis=1).astype(BF16).reshape(bb, CHUNK, 4 * LANES)

    old_k = kbuf[:, CHUNK:A_BAND, :]
    kbuf[:, 0:A_BAND - CHUNK, :] = old_k
    kbuf[:, A_BAND - CHUNK:A_BAND, :] = masked_variants(AB_K0)
    old_v = vbuf[:, CHUNK:A_BAND, :]
    vbuf[:, 0:A_BAND - CHUNK, :] = old_v
    vbuf[:, A_BAND - CHUNK:A_BAND, :] = masked_variants(AB_V0)

    qi = lax.broadcasted_iota(jnp.int32, (2 * CHUNK, A_BAND), 0)
    ki = lax.broadcasted_iota(jnp.int32, (2 * CHUNK, A_BAND), 1)
    first_tile = qi < CHUNK
    dist = jnp.abs(jnp.where(first_tile, qi, qi - CHUNK) + (A_BAND - CHUNK) - ki).astype(F32)
    valid = ki >= (A_BAND - CHUNK) - CHUNK * n
    row_first = lax.broadcasted_iota(jnp.int32, (2 * CHUNK, 1), 0) < CHUNK
    scale = A_HEAD_DIM ** -0.5

    def head_consts(kv, parity):
        ha = 4 * kv + parity
        hb = ha + 2
        slope = jnp.where(row_first, 2.0 ** (-8.0 * (ha + 1) / A_HEADS), 2.0 ** (-8.0 * (hb + 1) / A_HEADS))
        sink = jnp.where(row_first, sinks_ref[ha], sinks_ref[hb])
        return slope * dist, sink

    consts = [[head_consts(kv, par) for par in range(2)] for kv in range(A_KV_HEADS)]

    def attend(b, carry):
        r0 = pl.multiple_of(b * CHUNK, CHUNK)
        rsl = pl.ds(r0, CHUNK)
        combos = [(kv, par) for kv in range(A_KV_HEADS) for par in range(2)]
        q2 = []
        for kv in range(A_KV_HEADS):
            c0 = AB_Q0 + 2 * LANES * kv
            qq = jnp.concatenate([proj_scr[rsl, c0:c0 + LANES], proj_scr[rsl, c0 + LANES:c0 + 2 * LANES]], axis=0)
            q2.append((qq * scale).astype(BF16))
        s = [_dot_nt(q2[kv], kbuf[b, :, (2 * kv + par) * LANES:(2 * kv + par + 1) * LANES]) for kv, par in combos]
        s = [jnp.where(valid, s[i] - consts[kv][par][0], NEG) for i, (kv, par) in enumerate(combos)]
        m = [jnp.maximum(jnp.max(s[i], axis=-1, keepdims=True), consts[kv][par][1]) for i, (kv, par) in enumerate(combos)]
        pr = [jnp.exp(s[i] - m[i]) for i in range(4)]
        den = [jnp.sum(pr[i], axis=-1, keepdims=True) + jnp.exp(consts[kv][par][1] - m[i])
               for i, (kv, par) in enumerate(combos)]
        pv = [_dot(pr[i].astype(BF16), vbuf[b, :, (2 * kv + par) * LANES:(2 * kv + par + 1) * LANES])
              for i, (kv, par) in enumerate(combos)]
        for kv in range(A_KV_HEADS):
            c0 = AB_Q0 + 2 * LANES * kv
            acc = pv[2 * kv] / den[2 * kv] + pv[2 * kv + 1] / den[2 * kv + 1]
            y_scr[rsl, c0:c0 + LANES] = acc[0:CHUNK]
            y_scr[rsl, c0 + LANES:c0 + 2 * LANES] = acc[CHUNK:2 * CHUNK]
        return carry

    lax.fori_loop(0, bb, attend, 0)

    tail = xpad[:, CHUNK:CHUNK + SUBLANES, :]
    xpad[:, 0:SUBLANES, :] = tail
    xpad[:, SUBLANES:SUBLANES + CHUNK, :] = proj_scr[:, AB_X0:AB_X0 + B_WIDTH].reshape(bb, CHUNK, B_WIDTH)
    conv = convb_ref[...].reshape(1, 1, B_WIDTH)
    for j in range(B_CONV):
        off = SUBLANES - (B_CONV - 1) + j
        conv = conv + convw_ref[j:j + 1, :].reshape(1, 1, B_WIDTH) * xpad[:, off:off + CHUNK, :]
    c2 = conv.reshape(rows, B_WIDTH)
    cb = c2.astype(BF16)
    r = jax.nn.sigmoid(_dot(cb, wa_ref[...]) + ba_ref[...])
    i = jax.nn.sigmoid(_dot(cb, wx_ref[...]) + bx_ref[...])
    log_a = (-RG_C) * r * _softplus(-lam_ref[...])
    a = jnp.exp(log_a).reshape(bb, CHUNK, B_WIDTH)
    u = (jnp.sqrt(_neg_expm1(2.0 * log_a)) * (i * c2)).reshape(bb, CHUNK, B_WIDTH)
    ti = lax.broadcasted_iota(jnp.int32, (bb, CHUNK, B_WIDTH), 1)
    step = 1
    while step < CHUNK:
        keep = ti >= step
        a_prev = jnp.where(keep, pltpu.roll(a, step, 1), 1.0)
        u_prev = jnp.where(keep, pltpu.roll(u, step, 1), 0.0)
        u = u + a * u_prev
        a = a * a_prev
        step *= 2
    h = u + a * hcarry[...].reshape(bb, 1, B_WIDTH)
    hcarry[...] = h[:, CHUNK - 1, :]
    gate = jax.nn.gelu(proj_scr[:, AB_G0:AB_G0 + B_WIDTH], approximate=True)
    y_scr[:, A_WIDTH:A_WIDTH + B_WIDTH] = h.reshape(rows, B_WIDTH) * gate

    y = _dot(y_scr[...].astype(BF16), wout_ref[...])
    out = _layer_norm(DN_ALPHA * x + y, lng_ref[...], lnb_ref[...])
    o_ref[...] = out.reshape(bb, CHUNK, d_model)


def _mixer_ab(x3, win, sinks, convw, convb, wa, ba, wx, bx, lam, wout, lng, lnb):
    b, s, d = x3.shape
    bb = min(MIX_BATCH, b)
    assert b % bb == 0 and s % CHUNK == 0
    rows = bb * CHUNK
    tile = pl.BlockSpec((bb, CHUNK, d), lambda i, j: (i, j, 0))
    in_specs = [pl.BlockSpec(memory_space=pltpu.SMEM), tile, _resident(win.shape),
                _resident(convw.shape), _resident(convb.shape), _resident(wa.shape), _resident(ba.shape),
                _resident(wx.shape), _resident(bx.shape), _resident(lam.shape), _resident(wout.shape),
                _resident(lng.shape), _resident(lnb.shape)]
    return pl.pallas_call(
        functools.partial(_mixer_ab_body, bb=bb),
        out_shape=jax.ShapeDtypeStruct((b, s, d), F32),
        grid=(b // bb, s // CHUNK),
        in_specs=in_specs,
        out_specs=tile,
        scratch_shapes=[
            pltpu.VMEM((rows, AB_COLS), F32),
            pltpu.VMEM((bb, A_BAND, 4 * LANES), BF16),
            pltpu.VMEM((bb, A_BAND, 4 * LANES), BF16),
            pltpu.VMEM((bb, CHUNK + SUBLANES, B_WIDTH), F32),
            pltpu.VMEM((bb, B_WIDTH), F32),
            pltpu.VMEM((rows, A_WIDTH + B_WIDTH), F32),
        ],
        compiler_params=pltpu.CompilerParams(dimension_semantics=("arbitrary", "arbitrary"),
                                             vmem_limit_bytes=VMEM_LIMIT),
        name="mixer_ab",
    )(sinks, x3, win, convw, convb, wa, ba, wx, bx, lam, wout, lng, lnb)


C_GATE_COL = C_HEADS


def _mixer_c_body(x_ref, w1_ref, w2_ref, convw_ref, alog_ref, dtb_ref, ng_ref, wout_ref, lng_ref, lnb_ref,
                  o_ref, xpad, qkv_scr, z_scr, ba_scr, gcol_scr, grow_scr, state, og_scr, *, bb, group):
    n = pl.program_id(1)
    rows = bb * CHUNK
    d_model = x_ref.shape[-1]
    qkv_w = 3 * C_WIDTH

    @pl.when(n == 0)
    def _():
        xpad[...] = jnp.zeros_like(xpad)
        state[...] = jnp.zeros_like(state)

    x = x_ref[...].reshape(rows, d_model)
    xb = x.astype(BF16)
    tail = xpad[:, CHUNK:CHUNK + SUBLANES, :]
    xpad[:, 0:SUBLANES, :] = tail
    xpad[:, SUBLANES:SUBLANES + CHUNK, :] = _dot(xb, w1_ref[:, 0:qkv_w]).reshape(bb, CHUNK, qkv_w)
    z_scr[...] = _dot(xb, w1_ref[:, qkv_w:qkv_w + C_WIDTH])
    ba_scr[...] = _dot(xb, w2_ref[...])

    conv = jnp.zeros((bb, CHUNK, qkv_w), F32)
    for j in range(C_CONV):
        off = SUBLANES - (C_CONV - 1) + j
        conv = conv + convw_ref[j:j + 1, :].reshape(1, 1, qkv_w) * xpad[:, off:off + CHUNK, :]
    qkv_scr[...] = _silu(conv).reshape(rows, qkv_w)

    ri = lax.broadcasted_iota(jnp.int32, (CHUNK, CHUNK), 0)
    ci = lax.broadcasted_iota(jnp.int32, (CHUNK, CHUNK), 1)
    tril = ri >= ci
    strict = ri > ci
    eye = (ri == ci).astype(F32)
    tril_f = tril.astype(F32)
    triu_f = (ri <= ci).astype(F32)
    neg_rate = -jnp.exp(alog_ref[...])
    dtb = dtb_ref[...]
    norm_g = ng_ref[...]
    qscale = C_HEAD_DIM ** -0.5

    bab = ba_scr[...]
    g_all = neg_rate * _softplus(bab + dtb)
    ba_scr[...] = jax.nn.sigmoid(bab)
    for b in range(bb):
        gb = g_all[b * CHUNK:(b + 1) * CHUNK]
        gcol_scr[b * CHUNK:(b + 1) * CHUNK, :] = jnp.dot(tril_f, gb, precision=lax.Precision.HIGHEST,
                                                         preferred_element_type=F32)
        grow_scr[b * LANES:(b + 1) * LANES, :] = jnp.dot(gb.T, triu_f, precision=lax.Precision.HIGHEST,
                                                         preferred_element_type=F32)

    def unit_norm(t, extra):
        return t * (lax.rsqrt(jnp.sum(t * t, axis=-1, keepdims=True) + NORM_EPS) * extra)

    def head_cols(part, h):
        return slice(part * C_WIDTH + h * C_HEAD_DIM, part * C_WIDTH + (h + 1) * C_HEAD_DIM)

    def unit(it, carry):
        units = [(it * group + s, h) for s in range(group) for h in range(C_HEADS)]
        idx = range(len(units))
        rsl = [pl.ds(pl.multiple_of(b * CHUNK, CHUNK), CHUNK) for b, _ in units]
        st = [state[b * C_HEADS + h] for b, h in units]
        gate_col = [C_GATE_COL + h for _, h in units]
        beta = [ba_scr[rsl[i], h:h + 1] for i, (_, h) in enumerate(units)]
        gc = [gcol_scr[rsl[i], gate_col[i]:gate_col[i] + 1] for i in idx]
        gr = [grow_scr[pl.ds(pl.multiple_of(b * LANES, LANES) + gate_col[i], 1), :]
              for i, (b, _) in enumerate(units)]
        gl = [gc[i][CHUNK - 1:CHUNK, :] for i in idx]
        q = [unit_norm(qkv_scr[rsl[i], head_cols(0, h)], qscale) for i, (_, h) in enumerate(units)]
        k = [unit_norm(qkv_scr[rsl[i], head_cols(1, h)], 1.0) for i, (_, h) in enumerate(units)]
        v = [qkv_scr[rsl[i], head_cols(2, h)] for i, (_, h) in enumerate(units)]
        kb = [k[i] * beta[i] for i in idx]
        eg = [jnp.exp(gc[i]) for i in idx]
        qk = [_dot_nt(jnp.concatenate([q[i], kb[i]], axis=0).astype(BF16), k[i].astype(BF16)) for i in idx]
        decay = [jnp.where(tril, jnp.exp(jnp.where(tril, gc[i] - gr[i], 0.0)), 0.0) for i in idx]
        attn = [(qk[i][0:CHUNK] * decay[i]).astype(BF16) for i in idx]
        lmat = [jnp.where(strict, qk[i][CHUNK:2 * CHUNK] * decay[i], 0.0) for i in idx]
        tinv = [eye - lmat[i] for i in idx]
        pw = [lmat[i].astype(BF16) for i in idx]
        for _ in range(5):
            pw = [_dot(pw[i], pw[i]).astype(BF16) for i in idx]
            tinv = [tinv[i] + _dot(tinv[i].astype(BF16), pw[i]) for i in idx]
        rhs = [jnp.concatenate([v[i] * beta[i], kb[i] * eg[i]], axis=1).astype(BF16) for i in idx]
        uw = [_dot(tinv[i].astype(BF16), rhs[i]) for i in idx]
        wq = [_dot(jnp.concatenate([uw[i][:, C_HEAD_DIM:], q[i] * eg[i]], axis=0).astype(BF16), st[i].astype(BF16))
              for i in idx]
        vnb = [(uw[i][:, 0:C_HEAD_DIM] - wq[i][0:CHUNK]).astype(BF16) for i in idx]
        o = [wq[i][CHUNK:2 * CHUNK] + _dot(attn[i], vnb[i]) for i in idx]
        kdec = [(k[i] * jnp.exp(gl[i] - gc[i])).astype(BF16) for i in idx]
        new_st = [st[i] * jnp.exp(gl[i]) + _dot_tn(kdec[i], vnb[i]) for i in idx]
        for i, (b, h) in enumerate(units):
            state[b * C_HEADS + h] = new_st[i]
            on = o[i] * lax.rsqrt(jnp.mean(o[i] * o[i], axis=-1, keepdims=True) + NORM_EPS) * norm_g
            og_scr[rsl[i], head_cols(0, h)] = on * _silu(z_scr[rsl[i], head_cols(0, h)])
        return carry

    lax.fori_loop(0, bb // group, unit, 0)

    y = _dot(og_scr[...].astype(BF16), wout_ref[...])
    out = _layer_norm(DN_ALPHA * x + y, lng_ref[...], lnb_ref[...])
    o_ref[...] = out.reshape(bb, CHUNK, d_model)


def _mixer_c(x3, w1, w2, convw, alog, dtb, ng, wout, lng, lnb):
    b, s, d = x3.shape
    bb = min(MIX_BATCH, b)
    group = min(C_GROUP, bb)
    assert b % bb == 0 and s % CHUNK == 0 and bb % group == 0
    rows = bb * CHUNK
    tile = pl.BlockSpec((bb, CHUNK, d), lambda i, j: (i, j, 0))
    in_specs = [tile] + [_resident(a.shape) for a in (w1, w2, convw, alog, dtb, ng, wout, lng, lnb)]
    return pl.pallas_call(
        functools.partial(_mixer_c_body, bb=bb, group=group),
        out_shape=jax.ShapeDtypeStruct((b, s, d), F32),
        grid=(b // bb, s // CHUNK),
        in_specs=in_specs,
        out_specs=tile,
        scratch_shapes=[
            pltpu.VMEM((bb, CHUNK + SUBLANES, 3 * C_WIDTH), F32),
            pltpu.VMEM((rows, 3 * C_WIDTH), F32),
            pltpu.VMEM((rows, C_WIDTH), F32),
            pltpu.VMEM((rows, LANES), F32),
            pltpu.VMEM((rows, LANES), F32),
            pltpu.VMEM((bb * LANES, CHUNK), F32),
            pltpu.VMEM((bb * C_HEADS, C_HEAD_DIM, C_HEAD_DIM), F32),
            pltpu.VMEM((rows, C_WIDTH), F32),
        ],
        compiler_params=pltpu.CompilerParams(dimension_semantics=("arbitrary", "arbitrary"),
                                             vmem_limit_bytes=VMEM_LIMIT),
        name="mixer_c",
    )(x3, w1, w2, convw, alog, dtb, ng, wout, lng, lnb)


def _block_diag(w):
    nb, bs, _ = w.shape
    out = jnp.zeros((nb * bs, nb * bs), w.dtype)
    for i in range(nb):
        out = out.at[i * bs:(i + 1) * bs, i * bs:(i + 1) * bs].set(w[i])
    return out


def _row(v):
    return v.reshape(1, -1).astype(F32)


def kernel(x, p, ffn1_wg, ffn1_wu, ffn1_wd, ffn2_wg, ffn2_wu, ffn2_wd, ln_g, ln_b, ple_wg, ple_bg, ple_wp, ab_w_in, a_sinks, b_conv_w, b_conv_b, b_wa, b_ba, b_wx, b_bx, b_lam, ab_w_out, c_w_in, c_conv_w, c_a_log, c_dt_bias, c_norm_g, c_w_out):
    b, s, d = x.shape
    n = b * s
    for i in range(DEPTH):
        j = i // 2
        x = _ffn(x.reshape(n, d), ffn1_wg[i].astype(BF16), ffn1_wu[i].astype(BF16), ffn1_wd[i].astype(BF16),
                 _row(ln_g[i, 0]), _row(ln_b[i, 0])).reshape(b, s, d)
        if i % 2 == 0:
            w = ab_w_in[j]
            o1, o2, o3 = A_WIDTH, A_WIDTH + A_KV_WIDTH, A_WIDTH + 2 * A_KV_WIDTH
            hd = A_HEAD_DIM
            swap = lambda m: jnp.concatenate([m[:, hd:2 * hd], m[:, 0:hd]], axis=1)
            win = jnp.concatenate([w[:, :o1], w[:, o1:o2], swap(w[:, o1:o2]), w[:, o2:o3], swap(w[:, o2:o3]),
                                   w[:, o3:]], axis=1).astype(BF16)
            x = _mixer_ab(x, win, a_sinks[j].astype(F32), b_conv_w[j], _row(b_conv_b[j]),
                          _block_diag(b_wa[j]).astype(BF16), _row(b_ba[j]),
                          _block_diag(b_wx[j]).astype(BF16), _row(b_bx[j]), _row(b_lam[j]),
                          ab_w_out[j].astype(BF16), _row(ln_g[i, 1]), _row(ln_b[i, 1]))
        else:
            w = c_w_in[j]
            w1 = w[:, :4 * C_WIDTH].astype(BF16)
            w2 = jnp.pad(w[:, 4 * C_WIDTH:], ((0, 0), (0, LANES - 2 * C_HEADS))).astype(BF16)
            pad_gate = lambda v: jnp.pad(v.astype(F32), (C_GATE_COL, LANES - C_GATE_COL - C_HEADS)).reshape(1, LANES)
            x = _mixer_c(x, w1, w2, c_conv_w[j], pad_gate(c_a_log[j]), pad_gate(c_dt_bias[j]),
                         _row(c_norm_g[j]), c_w_out[j].astype(BF16), _row(ln_g[i, 1]), _row(ln_b[i, 1]))
        x = _ffn(x.reshape(n, d), ffn2_wg[i].astype(BF16), ffn2_wu[i].astype(BF16), ffn2_wd[i].astype(BF16),
                 _row(ln_g[i, 2]), _row(ln_b[i, 2]),
                 ple_args=(p[i].reshape(n, -1), ple_wg[i].astype(BF16), _row(ple_bg[i]),
                           ple_wp[i].astype(BF16))).reshape(b, s, d)
    return x
```

```python
import functools

import jax
import jax.numpy as jnp
from jax import lax
from jax.experimental import pallas as pl
from jax.experimental.pallas import tpu as pltpu

F32 = jnp.float32
BF16 = jnp.bfloat16

DEPTH = 2
CHUNK = 64
A_HEADS = 8
A_KV_HEADS = 2
A_HEAD_DIM = 64
A_WIDTH = A_HEADS * A_HEAD_DIM
A_KV_WIDTH = A_KV_HEADS * A_HEAD_DIM
A_PREV_CHUNKS = 2
A_BAND = (A_PREV_CHUNKS + 1) * CHUNK
B_WIDTH = 512
B_BLOCKS = 8
B_CONV = 4
RG_C = 8.0
C_HEADS = 8
C_HEAD_DIM = 128
C_WIDTH = C_HEADS * C_HEAD_DIM
C_CONV = 4
DN_ALPHA = (2.0 * DEPTH) ** 0.25
LN_EPS = 1e-5
NORM_EPS = 1e-6
NEG = -1e30

LANES = 128
SUBLANES = 8
FFN_ROWS = 1024
FFN_COLS = 256
MIX_BATCH = 8
C_GROUP = 4
CONV_COLS = 512
VMEM_LIMIT = 56 * 1024 * 1024


def _dot(a, b):
    return jnp.dot(a, b, preferred_element_type=F32)


def _dot_nt(a, b):
    return lax.dot_general(a, b, (((1,), (1,)), ((), ())), preferred_element_type=F32)


def _dot_tn(a, b):
    return lax.dot_general(a, b, (((0,), (0,)), ((), ())), preferred_element_type=F32)


def _layer_norm(z, g, b):
    mu = jnp.mean(z, axis=-1, keepdims=True)
    d = z - mu
    var = jnp.mean(d * d, axis=-1, keepdims=True)
    return d * lax.rsqrt(var + LN_EPS) * g + b


def _silu(x):
    return x * jax.nn.sigmoid(x)


def _softplus(x):
    e = jnp.exp(-jnp.abs(x))
    u = 1.0 + e
    log1p_e = jnp.where(u == 1.0, e, jnp.log(u) * (e / (u - 1.0)))
    return jnp.maximum(x, 0.0) + log1p_e


def _neg_expm1(y):
    return -jnp.tanh(0.5 * y) * (jnp.exp(y) + 1.0)


def _causal_conv(cur, tail, w):
    taps = w.shape[0]
    full = jnp.concatenate([tail, cur], axis=1)
    out = w[taps - 1:taps].reshape(1, 1, -1) * cur
    for d in range(1, taps):
        shifted = pltpu.roll(full, d, 1)[:, SUBLANES:, :]
        out = out + w[taps - 1 - d:taps - d].reshape(1, 1, -1) * shifted
    return out


def _resident(shape):
    zeros = (0,) * len(shape)
    return pl.BlockSpec(shape, lambda *_: zeros, pipeline_mode=pl.Buffered(1))


def _ffn_body(x_ref, wg_ref, wu_ref, wd_ref, lng_ref, lnb_ref, *rest, ple, n_chunks):
    if ple:
        p_ref, pwg_ref, pbg_ref, pwp_ref, o_ref, h_scr = rest
    else:
        o_ref, h_scr = rest
    x = x_ref[...]
    xb = x.astype(BF16)
    for c in range(n_chunks):
        sl = slice(c * FFN_COLS, (c + 1) * FFN_COLS)
        g = _dot(xb, wg_ref[:, sl])
        u = _dot(xb, wu_ref[:, sl])
        h_scr[:, sl] = (_silu(g) * u).astype(BF16)
    y = _dot(h_scr[...], wd_ref[...])
    out = _layer_norm(DN_ALPHA * x + 0.5 * y, lng_ref[...], lnb_ref[...])
    if ple:
        gate = jax.nn.sigmoid(_dot(out.astype(BF16), pwg_ref[...]) + pbg_ref[...])
        out = out + gate * _dot(p_ref[...].astype(BF16), pwp_ref[...])
    o_ref[...] = out


def _ffn(x2, wg, wu, wd, lng, lnb, ple_args=None):
    n, d = x2.shape
    f = wg.shape[1]
    rows = min(FFN_ROWS, n)
    assert n % rows == 0 and f % FFN_COLS == 0
    row_spec = pl.BlockSpec((rows, d), lambda i: (i, 0))
    in_specs = [row_spec, _resident((d, f)), _resident((d, f)), _resident((f, d)),
                _resident((1, d)), _resident((1, d))]
    args = [x2, wg, wu, wd, lng, lnb]
    if ple_args is not None:
        p_all, layer, pwg, pbg, pwp = ple_args
        dp = p_all.shape[-1]
        in_specs += [pl.BlockSpec((None, rows, dp), lambda i: (layer, i, 0)), _resident((d, d)),
                     _resident((1, d)), _resident((dp, d))]
        args += [p_all, pwg, pbg, pwp]
    return pl.pallas_call(
        functools.partial(_ffn_body, ple=ple_args is not None, n_chunks=f // FFN_COLS),
        out_shape=jax.ShapeDtypeStruct((n, d), F32),
        grid=(n // rows,),
        in_specs=in_specs,
        out_specs=row_spec,
        scratch_shapes=[pltpu.VMEM((rows, f), BF16)],
        compiler_params=pltpu.CompilerParams(dimension_semantics=("arbitrary",),
                                             vmem_limit_bytes=VMEM_LIMIT),
        name="ffn_ple" if ple_args is not None else "ffn",
    )(*args)


AB_Q0 = 0
AB_K0 = A_WIDTH
AB_V0 = AB_K0 + 2 * A_KV_WIDTH
AB_X0 = AB_V0 + 2 * A_KV_WIDTH
AB_G0 = AB_X0 + B_WIDTH
AB_COLS = AB_G0 + B_WIDTH


def _mixer_ab_body(sinks_ref, x_ref, win_ref, convw_ref, convb_ref, wa_ref, ba_ref, wx_ref, bx_ref,
                   lam_ref, wout_ref, lng_ref, lnb_ref, o_ref,
                   proj_scr, kbuf, vbuf, ctail, hcarry, y_scr, *, bb):
    n = pl.program_id(1)
    rows = bb * CHUNK
    d_model = x_ref.shape[-1]

    @pl.when(n == 0)
    def _():
        kbuf[...] = jnp.zeros_like(kbuf)
        vbuf[...] = jnp.zeros_like(vbuf)
        ctail[...] = jnp.zeros_like(ctail)
        hcarry[...] = jnp.zeros_like(hcarry)

    x = x_ref[...].reshape(rows, d_model)
    proj_scr[...] = _dot(x.astype(BF16), win_ref[...])

    lo = lax.broadcasted_iota(jnp.int32, (rows, LANES), 1) < A_HEAD_DIM

    def masked_variants(c0):
        ta = proj_scr[:, c0:c0 + LANES]
        tb = proj_scr[:, c0 + LANES:c0 + 2 * LANES]
        zero = jnp.zeros_like(ta)
        parts = [jnp.where(lo, ta, zero), jnp.where(lo, zero, tb),
                 jnp.where(lo, tb, zero), jnp.where(lo, zero, ta)]
        return jnp.concatenate(parts, axis=1).astype(BF16).reshape(bb, CHUNK, 4 * LANES)

    old_k = kbuf[:, CHUNK:A_BAND, :]
    kbuf[:, 0:A_BAND - CHUNK, :] = old_k
    kbuf[:, A_BAND - CHUNK:A_BAND, :] = masked_variants(AB_K0)
    old_v = vbuf[:, CHUNK:A_BAND, :]
    vbuf[:, 0:A_BAND - CHUNK, :] = old_v
    vbuf[:, A_BAND - CHUNK:A_BAND, :] = masked_variants(AB_V0)

    qi = lax.broadcasted_iota(jnp.int32, (2 * CHUNK, A_BAND), 0)
    ki = lax.broadcasted_iota(jnp.int32, (2 * CHUNK, A_BAND), 1)
    first_tile = qi < CHUNK
    dist = jnp.abs(jnp.where(first_tile, qi, qi - CHUNK) + (A_BAND - CHUNK) - ki).astype(F32)
    valid = ki >= (A_BAND - CHUNK) - CHUNK * n
    row_first = lax.broadcasted_iota(jnp.int32, (2 * CHUNK, 1), 0) < CHUNK
    scale = A_HEAD_DIM ** -0.5

    def head_consts(kv, parity):
        ha = 4 * kv + parity
        hb = ha + 2
        slope = jnp.where(row_first, 2.0 ** (-8.0 * (ha + 1) / A_HEADS), 2.0 ** (-8.0 * (hb + 1) / A_HEADS))
        sink = jnp.where(row_first, sinks_ref[ha], sinks_ref[hb])
        return slope * dist, sink

    consts = [[head_consts(kv, par) for par in range(2)] for kv in range(A_KV_HEADS)]

    def attend(b, carry):
        r0 = pl.multiple_of(b * CHUNK, CHUNK)
        rsl = pl.ds(r0, CHUNK)
        combos = [(kv, par) for kv in range(A_KV_HEADS) for par in range(2)]
        q2 = []
        for kv in range(A_KV_HEADS):
            c0 = AB_Q0 + 2 * LANES * kv
            qq = jnp.concatenate([proj_scr[rsl, c0:c0 + LANES], proj_scr[rsl, c0 + LANES:c0 + 2 * LANES]], axis=0)
            q2.append((qq * scale).astype(BF16))
        s = [_dot_nt(q2[kv], kbuf[b, :, (2 * kv + par) * LANES:(2 * kv + par + 1) * LANES]) for kv, par in combos]
        s = [jnp.where(valid, s[i] - consts[kv][par][0], NEG) for i, (kv, par) in enumerate(combos)]
        m = [jnp.maximum(jnp.max(s[i], axis=-1, keepdims=True), consts[kv][par][1]) for i, (kv, par) in enumerate(combos)]
        pr = [jnp.exp(s[i] - m[i]) for i in range(4)]
        den = [jnp.sum(pr[i], axis=-1, keepdims=True) + jnp.exp(consts[kv][par][1] - m[i])
               for i, (kv, par) in enumerate(combos)]
        pv = [_dot(pr[i].astype(BF16), vbuf[b, :, (2 * kv + par) * LANES:(2 * kv + par + 1) * LANES])
              for i, (kv, par) in enumerate(combos)]
        for kv in range(A_KV_HEADS):
            c0 = AB_Q0 + 2 * LANES * kv
            acc = pv[2 * kv] / den[2 * kv] + pv[2 * kv + 1] / den[2 * kv + 1]
            y_scr[rsl, c0:c0 + LANES] = acc[0:CHUNK]
            y_scr[rsl, c0 + LANES:c0 + 2 * LANES] = acc[CHUNK:2 * CHUNK]
        return carry

    lax.fori_loop(0, bb, attend, 0)

    bx3 = proj_scr[:, AB_X0:AB_X0 + B_WIDTH].reshape(bb, CHUNK, B_WIDTH)
    conv = _causal_conv(bx3, ctail[...], convw_ref[...]) + convb_ref[...].reshape(1, 1, B_WIDTH)
    ctail[...] = bx3[:, CHUNK - SUBLANES:CHUNK, :]
    c2 = conv.reshape(rows, B_WIDTH)
    cb = c2.astype(BF16)
    r = jax.nn.sigmoid(_dot(cb, wa_ref[...]) + ba_ref[...])
    i = jax.nn.sigmoid(_dot(cb, wx_ref[...]) + bx_ref[...])
    log_a = (-RG_C) * r * _softplus(-lam_ref[...])
    a = jnp.exp(log_a).reshape(bb, CHUNK, B_WIDTH)
    u = (jnp.sqrt(_neg_expm1(2.0 * log_a)) * (i * c2)).reshape(bb, CHUNK, B_WIDTH)
    ti = lax.broadcasted_iota(jnp.int32, (bb, CHUNK, B_WIDTH), 1)
    step = 1
    while step < SUBLANES:
        keep = ti >= step
        a_prev = jnp.where(keep, pltpu.roll(a, step, 1), 1.0)
        u_prev = jnp.where(keep, pltpu.roll(u, step, 1), 0.0)
        u = u + a * u_prev
        a = a * a_prev
        step *= 2
    while step < CHUNK:
        u_hi = u[:, step:, :] + a[:, step:, :] * u[:, :CHUNK - step, :]
        a_hi = a[:, step:, :] * a[:, :CHUNK - step, :]
        u = jnp.concatenate([u[:, :step, :], u_hi], axis=1)
        a = jnp.concatenate([a[:, :step, :], a_hi], axis=1)
        step *= 2
    h = u + a * hcarry[...].reshape(bb, 1, B_WIDTH)
    hcarry[...] = h[:, CHUNK - 1, :]
    gate = jax.nn.gelu(proj_scr[:, AB_G0:AB_G0 + B_WIDTH], approximate=True)
    y_scr[:, A_WIDTH:A_WIDTH + B_WIDTH] = h.reshape(rows, B_WIDTH) * gate

    y = _dot(y_scr[...].astype(BF16), wout_ref[...])
    out = _layer_norm(DN_ALPHA * x + y, lng_ref[...], lnb_ref[...])
    o_ref[...] = out.reshape(bb, CHUNK, d_model)


def _mixer_ab(x3, win, sinks, convw, convb, wa, ba, wx, bx, lam, wout, lng, lnb):
    b, s, d = x3.shape
    bb = min(MIX_BATCH, b)
    assert b % bb == 0 and s % CHUNK == 0
    rows = bb * CHUNK
    tile = pl.BlockSpec((bb, CHUNK, d), lambda i, j: (i, j, 0))
    in_specs = [pl.BlockSpec(memory_space=pltpu.SMEM), tile, _resident(win.shape),
                _resident(convw.shape), _resident(convb.shape), _resident(wa.shape), _resident(ba.shape),
                _resident(wx.shape), _resident(bx.shape), _resident(lam.shape), _resident(wout.shape),
                _resident(lng.shape), _resident(lnb.shape)]
    return pl.pallas_call(
        functools.partial(_mixer_ab_body, bb=bb),
        out_shape=jax.ShapeDtypeStruct((b, s, d), F32),
        grid=(b // bb, s // CHUNK),
        in_specs=in_specs,
        out_specs=tile,
        scratch_shapes=[
            pltpu.VMEM((rows, AB_COLS), F32),
            pltpu.VMEM((bb, A_BAND, 4 * LANES), BF16),
            pltpu.VMEM((bb, A_BAND, 4 * LANES), BF16),
            pltpu.VMEM((bb, SUBLANES, B_WIDTH), F32),
            pltpu.VMEM((bb, B_WIDTH), F32),
            pltpu.VMEM((rows, A_WIDTH + B_WIDTH), F32),
        ],
        compiler_params=pltpu.CompilerParams(dimension_semantics=("arbitrary", "arbitrary"),
                                             vmem_limit_bytes=VMEM_LIMIT),
        name="mixer_ab",
    )(sinks, x3, win, convw, convb, wa, ba, wx, bx, lam, wout, lng, lnb)


C_GATE_COL = C_HEADS


def _mixer_c_body(x_ref, w1_ref, w2_ref, convw_ref, alog_ref, dtb_ref, ng_ref, wout_ref, lng_ref, lnb_ref,
                  o_ref, ctail, qkv_scr, z_scr, ba_scr, gcol_scr, grow_scr, state, og_scr, *, bb, group):
    n = pl.program_id(1)
    rows = bb * CHUNK
    d_model = x_ref.shape[-1]
    qkv_w = 3 * C_WIDTH

    @pl.when(n == 0)
    def _():
        ctail[...] = jnp.zeros_like(ctail)
        state[...] = jnp.zeros_like(state)

    x = x_ref[...].reshape(rows, d_model)
    xb = x.astype(BF16)
    for c in range(qkv_w // CONV_COLS):
        csl = slice(c * CONV_COLS, (c + 1) * CONV_COLS)
        cur = _dot(xb, w1_ref[:, csl]).reshape(bb, CHUNK, CONV_COLS)
        conv = _causal_conv(cur, ctail[:, :, csl], convw_ref[:, csl])
        ctail[:, :, csl] = cur[:, CHUNK - SUBLANES:CHUNK, :]
        qkv_scr[:, csl] = _silu(conv).reshape(rows, CONV_COLS)
    z_scr[...] = _dot(xb, w1_ref[:, qkv_w:qkv_w + C_WIDTH])
    ba_scr[...] = _dot(xb, w2_ref[...])

    ri = lax.broadcasted_iota(jnp.int32, (CHUNK, CHUNK), 0)
    ci = lax.broadcasted_iota(jnp.int32, (CHUNK, CHUNK), 1)
    tril = ri >= ci
    strict = ri > ci
    eye = (ri == ci).astype(F32)
    tril_f = tril.astype(F32)
    triu_f = (ri <= ci).astype(F32)
    neg_rate = -jnp.exp(alog_ref[...])
    dtb = dtb_ref[...]
    norm_g = ng_ref[...]
    qscale = C_HEAD_DIM ** -0.5

    bab = ba_scr[...]
    g_all = neg_rate * _softplus(bab + dtb)
    ba_scr[...] = jax.nn.sigmoid(bab)
    for b in range(bb):
        gb = g_all[b * CHUNK:(b + 1) * CHUNK]
        gcol_scr[b * CHUNK:(b + 1) * CHUNK, :] = jnp.dot(tril_f, gb, precision=lax.Precision.HIGHEST,
                                                         preferred_element_type=F32)
        grow_scr[b * LANES:(b + 1) * LANES, :] = jnp.dot(gb.T, triu_f, precision=lax.Precision.HIGHEST,
                                                         preferred_element_type=F32)

    def unit_norm(t, extra):
        return t * (lax.rsqrt(jnp.sum(t * t, axis=-1, keepdims=True) + NORM_EPS) * extra)

    def head_cols(part, h):
        return slice(part * C_WIDTH + h * C_HEAD_DIM, part * C_WIDTH + (h + 1) * C_HEAD_DIM)

    def unit(it, carry):
        units = [(it * group + s, h) for s in range(group) for h in range(C_HEADS)]
        idx = range(len(units))
        rsl = [pl.ds(pl.multiple_of(b * CHUNK, CHUNK), CHUNK) for b, _ in units]
        st = [state[b * C_HEADS + h] for b, h in units]
        gate_col = [C_GATE_COL + h for _, h in units]
        beta = [ba_scr[rsl[i], h:h + 1] for i, (_, h) in enumerate(units)]
        gc = [gcol_scr[rsl[i], gate_col[i]:gate_col[i] + 1] for i in idx]
        gr = [grow_scr[pl.ds(pl.multiple_of(b * LANES, LANES) + gate_col[i], 1), :]
              for i, (b, _) in enumerate(units)]
        gl = [gc[i][CHUNK - 1:CHUNK, :] for i in idx]
        q = [unit_norm(qkv_scr[rsl[i], head_cols(0, h)], qscale) for i, (_, h) in enumerate(units)]
        k = [unit_norm(qkv_scr[rsl[i], head_cols(1, h)], 1.0) for i, (_, h) in enumerate(units)]
        v = [qkv_scr[rsl[i], head_cols(2, h)] for i, (_, h) in enumerate(units)]
        kb = [k[i] * beta[i] for i in idx]
        eg = [jnp.exp(gc[i]) for i in idx]
        qk = [_dot_nt(jnp.concatenate([q[i], kb[i]], axis=0).astype(BF16), k[i].astype(BF16)) for i in idx]
        decay = [jnp.where(tril, jnp.exp(jnp.where(tril, gc[i] - gr[i], 0.0)), 0.0) for i in idx]
        attn = [(qk[i][0:CHUNK] * decay[i]).astype(BF16) for i in idx]
        lmat = [jnp.where(strict, qk[i][CHUNK:2 * CHUNK] * decay[i], 0.0) for i in idx]
        tinv = [eye - lmat[i] for i in idx]
        pw = [lmat[i].astype(BF16) for i in idx]
        for _ in range(5):
            pw = [_dot(pw[i], pw[i]).astype(BF16) for i in idx]
            tinv = [tinv[i] + _dot(tinv[i].astype(BF16), pw[i]) for i in idx]
        rhs = [jnp.concatenate([v[i] * beta[i], kb[i] * eg[i]], axis=1).astype(BF16) for i in idx]
        uw = [_dot(tinv[i].astype(BF16), rhs[i]) for i in idx]
        wq = [_dot(jnp.concatenate([uw[i][:, C_HEAD_DIM:], q[i] * eg[i]], axis=0).astype(BF16), st[i].astype(BF16))
              for i in idx]
        vnb = [(uw[i][:, 0:C_HEAD_DIM] - wq[i][0:CHUNK]).astype(BF16) for i in idx]
        o = [wq[i][CHUNK:2 * CHUNK] + _dot(attn[i], vnb[i]) for i in idx]
        kdec = [(k[i] * jnp.exp(gl[i] - gc[i])).astype(BF16) for i in idx]
        new_st = [st[i] * jnp.exp(gl[i]) + _dot_tn(kdec[i], vnb[i]) for i in idx]
        for i, (b, h) in enumerate(units):
            state[b * C_HEADS + h] = new_st[i]
            on = o[i] * lax.rsqrt(jnp.mean(o[i] * o[i], axis=-1, keepdims=True) + NORM_EPS) * norm_g
            og_scr[rsl[i], head_cols(0, h)] = on * _silu(z_scr[rsl[i], head_cols(0, h)])
        return carry

    lax.fori_loop(0, bb // group, unit, 0)

    y = _dot(og_scr[...].astype(BF16), wout_ref[...])
    out = _layer_norm(DN_ALPHA * x + y, lng_ref[...], lnb_ref[...])
    o_ref[...] = out.reshape(bb, CHUNK, d_model)


def _mixer_c(x3, w1, w2, convw, alog, dtb, ng, wout, lng, lnb):
    b, s, d = x3.shape
    bb = min(MIX_BATCH, b)
    group = min(C_GROUP, bb)
    assert b % bb == 0 and s % CHUNK == 0 and bb % group == 0
    rows = bb * CHUNK
    tile = pl.BlockSpec((bb, CHUNK, d), lambda i, j: (i, j, 0))
    in_specs = [tile] + [_resident(a.shape) for a in (w1, w2, convw, alog, dtb, ng, wout, lng, lnb)]
    return pl.pallas_call(
        functools.partial(_mixer_c_body, bb=bb, group=group),
        out_shape=jax.ShapeDtypeStruct((b, s, d), F32),
        grid=(b // bb, s // CHUNK),
        in_specs=in_specs,
        out_specs=tile,
        scratch_shapes=[
            pltpu.VMEM((bb, SUBLANES, 3 * C_WIDTH), F32),
            pltpu.VMEM((rows, 3 * C_WIDTH), F32),
            pltpu.VMEM((rows, C_WIDTH), F32),
            pltpu.VMEM((rows, LANES), F32),
            pltpu.VMEM((rows, LANES), F32),
            pltpu.VMEM((bb * LANES, CHUNK), F32),
            pltpu.VMEM((bb * C_HEADS, C_HEAD_DIM, C_HEAD_DIM), F32),
            pltpu.VMEM((rows, C_WIDTH), F32),
        ],
        compiler_params=pltpu.CompilerParams(dimension_semantics=("arbitrary", "arbitrary"),
                                             vmem_limit_bytes=VMEM_LIMIT),
        name="mixer_c",
    )(x3, w1, w2, convw, alog, dtb, ng, wout, lng, lnb)


def _block_diag(w):
    nb, bs, _ = w.shape
    out = jnp.zeros((nb * bs, nb * bs), w.dtype)
    for i in range(nb):
        out = out.at[i * bs:(i + 1) * bs, i * bs:(i + 1) * bs].set(w[i])
    return out


def _row(v):
    return v.reshape(1, -1).astype(F32)


def kernel(x, p, ffn1_wg, ffn1_wu, ffn1_wd, ffn2_wg, ffn2_wu, ffn2_wd, ln_g, ln_b, ple_wg, ple_bg, ple_wp, ab_w_in, a_sinks, b_conv_w, b_conv_b, b_wa, b_ba, b_wx, b_bx, b_lam, ab_w_out, c_w_in, c_conv_w, c_a_log, c_dt_bias, c_norm_g, c_w_out):
    b, s, d = x.shape
    n = b * s
    for i in range(DEPTH):
        j = i // 2
        x = _ffn(x.reshape(n, d), ffn1_wg[i].astype(BF16), ffn1_wu[i].astype(BF16), ffn1_wd[i].astype(BF16),
                 _row(ln_g[i, 0]), _row(ln_b[i, 0])).reshape(b, s, d)
        if i % 2 == 0:
            w = ab_w_in[j]
            o1, o2, o3 = A_WIDTH, A_WIDTH + A_KV_WIDTH, A_WIDTH + 2 * A_KV_WIDTH
            hd = A_HEAD_DIM
            swap = lambda m: jnp.concatenate([m[:, hd:2 * hd], m[:, 0:hd]], axis=1)
            win = jnp.concatenate([w[:, :o1], w[:, o1:o2], swap(w[:, o1:o2]), w[:, o2:o3], swap(w[:, o2:o3]),
                                   w[:, o3:]], axis=1).astype(BF16)
            x = _mixer_ab(x, win, a_sinks[j].astype(F32), b_conv_w[j], _row(b_conv_b[j]),
                          _block_diag(b_wa[j]).astype(BF16), _row(b_ba[j]),
                          _block_diag(b_wx[j]).astype(BF16), _row(b_bx[j]), _row(b_lam[j]),
                          ab_w_out[j].astype(BF16), _row(ln_g[i, 1]), _row(ln_b[i, 1]))
        else:
            w = c_w_in[j]
            w1 = w[:, :4 * C_WIDTH].astype(BF16)
            w2 = jnp.pad(w[:, 4 * C_WIDTH:], ((0, 0), (0, LANES - 2 * C_HEADS))).astype(BF16)
            pad_gate = lambda v: jnp.pad(v.astype(F32), (C_GATE_COL, LANES - C_GATE_COL - C_HEADS)).reshape(1, LANES)
            x = _mixer_c(x, w1, w2, c_conv_w[j], pad_gate(c_a_log[j]), pad_gate(c_dt_bias[j]),
                         _row(c_norm_g[j]), c_w_out[j].astype(BF16), _row(ln_g[i, 1]), _row(ln_b[i, 1]))
        x = _ffn(x.reshape(n, d), ffn2_wg[i].astype(BF16), ffn2_wu[i].astype(BF16), ffn2_wd[i].astype(BF16),
                 _row(ln_g[i, 2]), _row(ln_b[i, 2]),
                 ple_args=(p.reshape(DEPTH, n, -1), i, ple_wg[i].astype(BF16), _row(ple_bg[i]),
                           ple_wp[i].astype(BF16))).reshape(b, s, d)
    return x
```

```python
import functools

import jax
import jax.numpy as jnp
from jax import lax
from jax.experimental import pallas as pl
from jax.experimental.pallas import tpu as pltpu

F32 = jnp.float32
BF16 = jnp.bfloat16

DEPTH = 2
CHUNK = 64
A_HEADS = 8
A_KV_HEADS = 2
A_HEAD_DIM = 64
A_WIDTH = A_HEADS * A_HEAD_DIM
A_KV_WIDTH = A_KV_HEADS * A_HEAD_DIM
A_PREV_CHUNKS = 2
A_BAND = (A_PREV_CHUNKS + 1) * CHUNK
B_WIDTH = 512
B_BLOCKS = 8
B_CONV = 4
RG_C = 8.0
C_HEADS = 8
C_HEAD_DIM = 128
C_WIDTH = C_HEADS * C_HEAD_DIM
C_CONV = 4
DN_ALPHA = (2.0 * DEPTH) ** 0.25
LN_EPS = 1e-5
NORM_EPS = 1e-6
NEG = -1e30

LANES = 128
SUBLANES = 8
FFN_ROWS = 1024
FFN_COLS = 256
FFN_OUT_ROWS = 256
MIX_BATCH = 8
C_GROUP = 8
CONV_COLS = 512
VMEM_LIMIT = 56 * 1024 * 1024


def _dot(a, b):
    return jnp.dot(a, b, preferred_element_type=F32)


def _dot_nt(a, b):
    return lax.dot_general(a, b, (((1,), (1,)), ((), ())), preferred_element_type=F32)


def _dot_tn(a, b):
    return lax.dot_general(a, b, (((0,), (0,)), ((), ())), preferred_element_type=F32)


def _layer_norm(z, g, b):
    mu = jnp.mean(z, axis=-1, keepdims=True)
    d = z - mu
    var = jnp.mean(d * d, axis=-1, keepdims=True)
    return d * lax.rsqrt(var + LN_EPS) * g + b


def _silu(x):
    return x * jax.nn.sigmoid(x)


def _softplus(x):
    e = jnp.exp(-jnp.abs(x))
    u = 1.0 + e
    log1p_e = jnp.where(u == 1.0, e, jnp.log(u) * (e / (u - 1.0)))
    return jnp.maximum(x, 0.0) + log1p_e


def _neg_expm1(y):
    return -jnp.tanh(0.5 * y) * (jnp.exp(y) + 1.0)


def _causal_conv(cur, tail, w):
    taps = w.shape[0]
    full = jnp.concatenate([tail, cur], axis=1)
    out = w[taps - 1:taps].reshape(1, 1, -1) * cur
    for d in range(1, taps):
        shifted = pltpu.roll(full, d, 1)[:, SUBLANES:, :]
        out = out + w[taps - 1 - d:taps - d].reshape(1, 1, -1) * shifted
    return out


def _resident(shape):
    zeros = (0,) * len(shape)
    return pl.BlockSpec(shape, lambda *_: zeros, pipeline_mode=pl.Buffered(1))


def _ffn_body(x_ref, wg_ref, wu_ref, wd_ref, lng_ref, lnb_ref, *rest, ple, n_chunks):
    if ple:
        p_ref, pwg_ref, pbg_ref, pwp_ref, o_ref, h_scr = rest
    else:
        o_ref, h_scr = rest
    xb = x_ref[...].astype(BF16)
    for c in range(n_chunks):
        sl = slice(c * FFN_COLS, (c + 1) * FFN_COLS)
        g = _dot(xb, wg_ref[:, sl])
        u = _dot(xb, wu_ref[:, sl])
        h_scr[:, sl] = (_silu(g) * u).astype(BF16)
    for r in range(x_ref.shape[0] // FFN_OUT_ROWS):
        rs = slice(r * FFN_OUT_ROWS, (r + 1) * FFN_OUT_ROWS)
        y = _dot(h_scr[rs, :], wd_ref[...])
        o_ref[rs, :] = _layer_norm(DN_ALPHA * x_ref[rs, :] + 0.5 * y, lng_ref[...], lnb_ref[...])
    if ple:
        out = o_ref[...]
        gate = jax.nn.sigmoid(_dot(out.astype(BF16), pwg_ref[...]) + pbg_ref[...])
        o_ref[...] = out + gate * _dot(p_ref[...].astype(BF16), pwp_ref[...])


def _ffn(x2, wg, wu, wd, lng, lnb, ple_args=None):
    n, d = x2.shape
    f = wg.shape[1]
    rows = min(FFN_ROWS, n)
    assert n % rows == 0 and f % FFN_COLS == 0 and rows % FFN_OUT_ROWS == 0
    row_spec = pl.BlockSpec((rows, d), lambda i: (i, 0))
    in_specs = [row_spec, _resident((d, f)), _resident((d, f)), _resident((f, d)),
                _resident((1, d)), _resident((1, d))]
    args = [x2, wg, wu, wd, lng, lnb]
    if ple_args is not None:
        p_all, layer, pwg, pbg, pwp = ple_args
        dp = p_all.shape[-1]
        in_specs += [pl.BlockSpec((None, rows, dp), lambda i: (layer, i, 0)), _resident((d, d)),
                     _resident((1, d)), _resident((dp, d))]
        args += [p_all, pwg, pbg, pwp]
    return pl.pallas_call(
        functools.partial(_ffn_body, ple=ple_args is not None, n_chunks=f // FFN_COLS),
        out_shape=jax.ShapeDtypeStruct((n, d), F32),
        grid=(n // rows,),
        in_specs=in_specs,
        out_specs=row_spec,
        scratch_shapes=[pltpu.VMEM((rows, f), BF16)],
        compiler_params=pltpu.CompilerParams(dimension_semantics=("arbitrary",),
                                             vmem_limit_bytes=VMEM_LIMIT),
        name="ffn_ple" if ple_args is not None else "ffn",
    )(*args)


AB_Q0 = 0
AB_K0 = A_WIDTH
AB_V0 = AB_K0 + 2 * A_KV_WIDTH
AB_X0 = AB_V0 + 2 * A_KV_WIDTH
AB_G0 = AB_X0 + B_WIDTH
AB_COLS = AB_G0 + B_WIDTH


def _mixer_ab_body(sinks_ref, x_ref, win_ref, convw_ref, convb_ref, wa_ref, ba_ref, wx_ref, bx_ref,
                   lam_ref, wout_ref, lng_ref, lnb_ref, o_ref,
                   proj_scr, kbuf, vbuf, ctail, hcarry, y_scr, *, bb):
    n = pl.program_id(1)
    rows = bb * CHUNK
    d_model = x_ref.shape[-1]

    @pl.when(n == 0)
    def _():
        kbuf[...] = jnp.zeros_like(kbuf)
        vbuf[...] = jnp.zeros_like(vbuf)
        ctail[...] = jnp.zeros_like(ctail)
        hcarry[...] = jnp.zeros_like(hcarry)

    x = x_ref[...].reshape(rows, d_model)
    proj_scr[...] = _dot(x.astype(BF16), win_ref[...])

    lo = lax.broadcasted_iota(jnp.int32, (rows, LANES), 1) < A_HEAD_DIM

    def masked_variants(c0):
        ta = proj_scr[:, c0:c0 + LANES]
        tb = proj_scr[:, c0 + LANES:c0 + 2 * LANES]
        zero = jnp.zeros_like(ta)
        parts = [jnp.where(lo, ta, zero), jnp.where(lo, zero, tb),
                 jnp.where(lo, tb, zero), jnp.where(lo, zero, ta)]
        return jnp.concatenate(parts, axis=1).astype(BF16).reshape(bb, CHUNK, 4 * LANES)

    old_k = kbuf[:, CHUNK:A_BAND, :]
    kbuf[:, 0:A_BAND - CHUNK, :] = old_k
    kbuf[:, A_BAND - CHUNK:A_BAND, :] = masked_variants(AB_K0)
    old_v = vbuf[:, CHUNK:A_BAND, :]
    vbuf[:, 0:A_BAND - CHUNK, :] = old_v
    vbuf[:, A_BAND - CHUNK:A_BAND, :] = masked_variants(AB_V0)

    qi = lax.broadcasted_iota(jnp.int32, (2 * CHUNK, A_BAND), 0)
    ki = lax.broadcasted_iota(jnp.int32, (2 * CHUNK, A_BAND), 1)
    first_tile = qi < CHUNK
    dist = jnp.abs(jnp.where(first_tile, qi, qi - CHUNK) + (A_BAND - CHUNK) - ki).astype(F32)
    valid = ki >= (A_BAND - CHUNK) - CHUNK * n
    row_first = lax.broadcasted_iota(jnp.int32, (2 * CHUNK, 1), 0) < CHUNK
    scale = A_HEAD_DIM ** -0.5

    def head_consts(kv, parity):
        ha = 4 * kv + parity
        hb = ha + 2
        slope = jnp.where(row_first, 2.0 ** (-8.0 * (ha + 1) / A_HEADS), 2.0 ** (-8.0 * (hb + 1) / A_HEADS))
        sink = jnp.where(row_first, sinks_ref[ha], sinks_ref[hb])
        return slope * dist, sink

    consts = [[head_consts(kv, par) for par in range(2)] for kv in range(A_KV_HEADS)]

    def attend(b, carry):
        r0 = pl.multiple_of(b * CHUNK, CHUNK)
        rsl = pl.ds(r0, CHUNK)
        combos = [(kv, par) for kv in range(A_KV_HEADS) for par in range(2)]
        q2 = []
        for kv in range(A_KV_HEADS):
            c0 = AB_Q0 + 2 * LANES * kv
            qq = jnp.concatenate([proj_scr[rsl, c0:c0 + LANES], proj_scr[rsl, c0 + LANES:c0 + 2 * LANES]], axis=0)
            q2.append((qq * scale).astype(BF16))
        s = [_dot_nt(q2[kv], kbuf[b, :, (2 * kv + par) * LANES:(2 * kv + par + 1) * LANES]) for kv, par in combos]
        s = [jnp.where(valid, s[i] - consts[kv][par][0], NEG) for i, (kv, par) in enumerate(combos)]
        m = [jnp.maximum(jnp.max(s[i], axis=-1, keepdims=True), consts[kv][par][1]) for i, (kv, par) in enumerate(combos)]
        pr = [jnp.exp(s[i] - m[i]) for i in range(4)]
        den = [jnp.sum(pr[i], axis=-1, keepdims=True) + jnp.exp(consts[kv][par][1] - m[i])
               for i, (kv, par) in enumerate(combos)]
        pv = [_dot(pr[i].astype(BF16), vbuf[b, :, (2 * kv + par) * LANES:(2 * kv + par + 1) * LANES])
              for i, (kv, par) in enumerate(combos)]
        for kv in range(A_KV_HEADS):
            c0 = AB_Q0 + 2 * LANES * kv
            acc = pv[2 * kv] / den[2 * kv] + pv[2 * kv + 1] / den[2 * kv + 1]
            y_scr[rsl, c0:c0 + LANES] = acc[0:CHUNK]
            y_scr[rsl, c0 + LANES:c0 + 2 * LANES] = acc[CHUNK:2 * CHUNK]
        return carry

    lax.fori_loop(0, bb, attend, 0)

    bx3 = proj_scr[:, AB_X0:AB_X0 + B_WIDTH].reshape(bb, CHUNK, B_WIDTH)
    conv = _causal_conv(bx3, ctail[...], convw_ref[...]) + convb_ref[...].reshape(1, 1, B_WIDTH)
    ctail[...] = bx3[:, CHUNK - SUBLANES:CHUNK, :]
    c2 = conv.reshape(rows, B_WIDTH)
    cb = c2.astype(BF16)
    r = jax.nn.sigmoid(_dot(cb, wa_ref[...]) + ba_ref[...])
    i = jax.nn.sigmoid(_dot(cb, wx_ref[...]) + bx_ref[...])
    log_a = (-RG_C) * r * _softplus(-lam_ref[...])
    a = jnp.exp(log_a).reshape(bb, CHUNK, B_WIDTH)
    u = (jnp.sqrt(_neg_expm1(2.0 * log_a)) * (i * c2)).reshape(bb, CHUNK, B_WIDTH)
    ti = lax.broadcasted_iota(jnp.int32, (bb, CHUNK, B_WIDTH), 1)
    step = 1
    while step < SUBLANES:
        keep = ti >= step
        a_prev = jnp.where(keep, pltpu.roll(a, step, 1), 1.0)
        u_prev = jnp.where(keep, pltpu.roll(u, step, 1), 0.0)
        u = u + a * u_prev
        a = a * a_prev
        step *= 2
    while step < CHUNK:
        u_hi = u[:, step:, :] + a[:, step:, :] * u[:, :CHUNK - step, :]
        a_hi = a[:, step:, :] * a[:, :CHUNK - step, :]
        u = jnp.concatenate([u[:, :step, :], u_hi], axis=1)
        a = jnp.concatenate([a[:, :step, :], a_hi], axis=1)
        step *= 2
    h = u + a * hcarry[...].reshape(bb, 1, B_WIDTH)
    hcarry[...] = h[:, CHUNK - 1, :]
    gate = jax.nn.gelu(proj_scr[:, AB_G0:AB_G0 + B_WIDTH], approximate=True)
    y_scr[:, A_WIDTH:A_WIDTH + B_WIDTH] = h.reshape(rows, B_WIDTH) * gate

    y = _dot(y_scr[...].astype(BF16), wout_ref[...])
    out = _layer_norm(DN_ALPHA * x + y, lng_ref[...], lnb_ref[...])
    o_ref[...] = out.reshape(bb, CHUNK, d_model)


def _mixer_ab(x3, win, sinks, convw, convb, wa, ba, wx, bx, lam, wout, lng, lnb):
    b, s, d = x3.shape
    bb = min(MIX_BATCH, b)
    assert b % bb == 0 and s % CHUNK == 0
    rows = bb * CHUNK
    tile = pl.BlockSpec((bb, CHUNK, d), lambda i, j: (i, j, 0))
    in_specs = [pl.BlockSpec(memory_space=pltpu.SMEM), tile, _resident(win.shape),
                _resident(convw.shape), _resident(convb.shape), _resident(wa.shape), _resident(ba.shape),
                _resident(wx.shape), _resident(bx.shape), _resident(lam.shape), _resident(wout.shape),
                _resident(lng.shape), _resident(lnb.shape)]
    return pl.pallas_call(
        functools.partial(_mixer_ab_body, bb=bb),
        out_shape=jax.ShapeDtypeStruct((b, s, d), F32),
        grid=(b // bb, s // CHUNK),
        in_specs=in_specs,
        out_specs=tile,
        scratch_shapes=[
            pltpu.VMEM((rows, AB_COLS), F32),
            pltpu.VMEM((bb, A_BAND, 4 * LANES), BF16),
            pltpu.VMEM((bb, A_BAND, 4 * LANES), BF16),
            pltpu.VMEM((bb, SUBLANES, B_WIDTH), F32),
            pltpu.VMEM((bb, B_WIDTH), F32),
            pltpu.VMEM((rows, A_WIDTH + B_WIDTH), F32),
        ],
        compiler_params=pltpu.CompilerParams(dimension_semantics=("arbitrary", "arbitrary"),
                                             vmem_limit_bytes=VMEM_LIMIT),
        name="mixer_ab",
    )(sinks, x3, win, convw, convb, wa, ba, wx, bx, lam, wout, lng, lnb)


C_GATE_COL = C_HEADS


def _mixer_c_body(x_ref, w1_ref, w2_ref, convw_ref, alog_ref, dtb_ref, ng_ref, wout_ref, lng_ref, lnb_ref,
                  o_ref, ctail, qkv_scr, z_scr, ba_scr, gcol_scr, grow_scr, state, og_scr, *, bb, group):
    n = pl.program_id(1)
    rows = bb * CHUNK
    d_model = x_ref.shape[-1]
    qkv_w = 3 * C_WIDTH

    @pl.when(n == 0)
    def _():
        ctail[...] = jnp.zeros_like(ctail)
        state[...] = jnp.zeros_like(state)

    x = x_ref[...].reshape(rows, d_model)
    xb = x.astype(BF16)
    for c in range(qkv_w // CONV_COLS):
        csl = slice(c * CONV_COLS, (c + 1) * CONV_COLS)
        cur = _dot(xb, w1_ref[:, csl]).reshape(bb, CHUNK, CONV_COLS)
        conv = _causal_conv(cur, ctail[:, :, csl], convw_ref[:, csl])
        ctail[:, :, csl] = cur[:, CHUNK - SUBLANES:CHUNK, :]
        qkv_scr[:, csl] = _silu(conv).reshape(rows, CONV_COLS)
    z_scr[...] = _dot(xb, w1_ref[:, qkv_w:qkv_w + C_WIDTH])
    ba_scr[...] = _dot(xb, w2_ref[...])

    ri = lax.broadcasted_iota(jnp.int32, (CHUNK, CHUNK), 0)
    ci = lax.broadcasted_iota(jnp.int32, (CHUNK, CHUNK), 1)
    tril = ri >= ci
    ri2 = lax.broadcasted_iota(jnp.int32, (CHUNK, LANES), 0)
    li2 = lax.broadcasted_iota(jnp.int32, (CHUNK, LANES), 1)
    ci2 = jnp.bitwise_and(li2, CHUNK - 1)
    lane_lo = li2 < CHUNK
    tril2 = ri2 >= ci2
    strict2 = ri2 > ci2
    eye2 = (ri2 == ci2).astype(F32)
    keep_lo = jnp.where(lane_lo, 1.0, 0.0).astype(BF16)
    keep_hi = jnp.where(lane_lo, 0.0, 1.0).astype(BF16)
    zero_t = jnp.zeros((CHUNK, C_HEAD_DIM), BF16)
    neg_rate = -jnp.exp(alog_ref[...])
    dtb = dtb_ref[...]
    norm_g = ng_ref[...]
    qscale = C_HEAD_DIM ** -0.5

    bab = ba_scr[...]
    g_all = neg_rate * _softplus(bab + dtb)
    ba_scr[...] = jax.nn.sigmoid(bab)
    g_hi = g_all.astype(BF16)
    rem = g_all - g_hi.astype(F32)
    g_mid = rem.astype(BF16)
    g_lo = (rem - g_mid.astype(F32)).astype(BF16)
    pieces = jnp.concatenate([part[b * CHUNK:(b + 1) * CHUNK] for part in (g_hi, g_mid, g_lo) for b in range(bb)],
                             axis=1)
    csum = _dot(tril.astype(BF16), pieces)
    for b in range(bb):
        gcb = (csum[:, b * LANES:(b + 1) * LANES] + csum[:, (bb + b) * LANES:(bb + b + 1) * LANES]
               + csum[:, (2 * bb + b) * LANES:(2 * bb + b + 1) * LANES])
        gcol_scr[b * CHUNK:(b + 1) * CHUNK, :] = gcb
        grow_scr[b * LANES:(b + 1) * LANES, :] = jnp.concatenate([gcb, gcb], axis=0).T

    def unit_norm(t, extra):
        return t * (lax.rsqrt(jnp.sum(t * t, axis=-1, keepdims=True) + NORM_EPS) * extra)

    def head_cols(part, h):
        return slice(part * C_WIDTH + h * C_HEAD_DIM, part * C_WIDTH + (h + 1) * C_HEAD_DIM)

    def lane_cat(a, b):
        return jnp.concatenate([a, b], axis=1)

    def block_diag2(top, bottom):
        return jnp.concatenate([lane_cat(top, jnp.zeros_like(top)), lane_cat(jnp.zeros_like(bottom), bottom)], axis=0)

    def unit(it, carry):
        units = [(it * group + s, p) for s in range(group) for p in range(C_HEADS // 2)]
        idx = range(len(units))
        rsl = [pl.ds(pl.multiple_of(b * CHUNK, CHUNK), CHUNK) for b, _ in units]
        heads = [(2 * p, 2 * p + 1) for _, p in units]
        st = [[state[b * C_HEADS + h] for h in heads[i]] for i, (b, _) in enumerate(units)]
        beta = [[ba_scr[rsl[i], h:h + 1] for h in heads[i]] for i in idx]
        gc = [[gcol_scr[rsl[i], C_GATE_COL + h:C_GATE_COL + h + 1] for h in heads[i]] for i in idx]
        gr = [[grow_scr[pl.ds(pl.multiple_of(b * LANES, LANES) + C_GATE_COL + h, 1), :] for h in heads[i]]
              for i, (b, _) in enumerate(units)]
        gl = [[gc[i][j][CHUNK - 1:CHUNK, :] for j in range(2)] for i in idx]
        eg = [[jnp.exp(gc[i][j]) for j in range(2)] for i in idx]
        q = [[unit_norm(qkv_scr[rsl[i], head_cols(0, h)], qscale) for h in heads[i]] for i in idx]
        k = [[unit_norm(qkv_scr[rsl[i], head_cols(1, h)], 1.0) for h in heads[i]] for i in idx]
        v = [[qkv_scr[rsl[i], head_cols(2, h)] for h in heads[i]] for i in idx]
        kb = [[k[i][j] * beta[i][j] for j in range(2)] for i in idx]
        lhs = [jnp.concatenate([lane_cat(q[i][0], q[i][1]), lane_cat(kb[i][0], kb[i][1])], axis=0).astype(BF16)
               for i in idx]
        kbd = [block_diag2(k[i][0].astype(BF16), k[i][1].astype(BF16)) for i in idx]
        qk = [_dot_nt(lhs[i], kbd[i]) for i in idx]
        gc2 = [jnp.where(lane_lo, gc[i][0], gc[i][1]) for i in idx]
        gr2 = [jnp.where(lane_lo[0:1], gr[i][0], gr[i][1]) for i in idx]
        decay = [jnp.where(tril2, jnp.exp(jnp.where(tril2, gc2[i] - gr2[i], 0.0)), 0.0) for i in idx]
        attn = [(qk[i][0:CHUNK] * decay[i]).astype(BF16) for i in idx]
        lmat = [jnp.where(strict2, qk[i][CHUNK:2 * CHUNK] * decay[i], 0.0) for i in idx]
        tinv = [eye2 - lmat[i] for i in idx]
        pw = [lmat[i].astype(BF16) for i in idx]
        for _ in range(5):
            pw = [_dot(pw[i], jnp.concatenate([pw[i] * keep_lo, pw[i] * keep_hi], axis=0)).astype(BF16) for i in idx]
            tinv = [tinv[i] + _dot(tinv[i].astype(BF16), jnp.concatenate([pw[i] * keep_lo, pw[i] * keep_hi], axis=0))
                    for i in idx]
        rhs = [jnp.concatenate([
            jnp.concatenate([(v[i][0] * beta[i][0]).astype(BF16), zero_t, (kb[i][0] * eg[i][0]).astype(BF16), zero_t], axis=1),
            jnp.concatenate([zero_t, (v[i][1] * beta[i][1]).astype(BF16), zero_t, (kb[i][1] * eg[i][1]).astype(BF16)], axis=1),
        ], axis=0) for i in idx]
        uw = [_dot(tinv[i].astype(BF16), rhs[i]) for i in idx]
        sbd = [block_diag2(st[i][0].astype(BF16), st[i][1].astype(BF16)) for i in idx]
        wq = [_dot(jnp.concatenate([uw[i][:, 2 * C_HEAD_DIM:],
                                    lane_cat(q[i][0] * eg[i][0], q[i][1] * eg[i][1])], axis=0).astype(BF16), sbd[i])
              for i in idx]
        vnb = [(uw[i][:, 0:2 * C_HEAD_DIM] - wq[i][0:CHUNK]).astype(BF16) for i in idx]
        vbd = [block_diag2(vnb[i][:, 0:C_HEAD_DIM], vnb[i][:, C_HEAD_DIM:]) for i in idx]
        o = [wq[i][CHUNK:2 * CHUNK] + _dot(attn[i], vbd[i]) for i in idx]
        for i, (b, _) in enumerate(units):
            for j, h in enumerate(heads[i]):
                hs = slice(j * C_HEAD_DIM, (j + 1) * C_HEAD_DIM)
                kdec = (k[i][j] * jnp.exp(gl[i][j] - gc[i][j])).astype(BF16)
                state[b * C_HEADS + h] = st[i][j] * jnp.exp(gl[i][j]) + _dot_tn(kdec, vnb[i][:, hs])
                oh = o[i][:, hs]
                on = oh * lax.rsqrt(jnp.mean(oh * oh, axis=-1, keepdims=True) + NORM_EPS) * norm_g
                og_scr[rsl[i], head_cols(0, h)] = on * _silu(z_scr[rsl[i], head_cols(0, h)])
        return carry

    lax.fori_loop(0, bb // group, unit, 0)

    y = _dot(og_scr[...].astype(BF16), wout_ref[...])
    out = _layer_norm(DN_ALPHA * x + y, lng_ref[...], lnb_ref[...])
    o_ref[...] = out.reshape(bb, CHUNK, d_model)


def _mixer_c(x3, w1, w2, convw, alog, dtb, ng, wout, lng, lnb):
    b, s, d = x3.shape
    bb = min(MIX_BATCH, b)
    group = min(C_GROUP, bb)
    assert b % bb == 0 and s % CHUNK == 0 and bb % group == 0
    rows = bb * CHUNK
    tile = pl.BlockSpec((bb, CHUNK, d), lambda i, j: (i, j, 0))
    in_specs = [tile] + [_resident(a.shape) for a in (w1, w2, convw, alog, dtb, ng, wout, lng, lnb)]
    return pl.pallas_call(
        functools.partial(_mixer_c_body, bb=bb, group=group),
        out_shape=jax.ShapeDtypeStruct((b, s, d), F32),
        grid=(b // bb, s // CHUNK),
        in_specs=in_specs,
        out_specs=tile,
        scratch_shapes=[
            pltpu.VMEM((bb, SUBLANES, 3 * C_WIDTH), F32),
            pltpu.VMEM((rows, 3 * C_WIDTH), F32),
            pltpu.VMEM((rows, C_WIDTH), F32),
            pltpu.VMEM((rows, LANES), F32),
            pltpu.VMEM((rows, LANES), F32),
            pltpu.VMEM((bb * LANES, LANES), F32),
            pltpu.VMEM((bb * C_HEADS, C_HEAD_DIM, C_HEAD_DIM), F32),
            pltpu.VMEM((rows, C_WIDTH), F32),
        ],
        compiler_params=pltpu.CompilerParams(dimension_semantics=("arbitrary", "arbitrary"),
                                             vmem_limit_bytes=VMEM_LIMIT),
        name="mixer_c",
    )(x3, w1, w2, convw, alog, dtb, ng, wout, lng, lnb)


def _block_diag(w):
    nb, bs, _ = w.shape
    out = jnp.zeros((nb * bs, nb * bs), w.dtype)
    for i in range(nb):
        out = out.at[i * bs:(i + 1) * bs, i * bs:(i + 1) * bs].set(w[i])
    return out


def _row(v):
    return v.reshape(1, -1).astype(F32)


def kernel(x, p, ffn1_wg, ffn1_wu, ffn1_wd, ffn2_wg, ffn2_wu, ffn2_wd, ln_g, ln_b, ple_wg, ple_bg, ple_wp, ab_w_in, a_sinks, b_conv_w, b_conv_b, b_wa, b_ba, b_wx, b_bx, b_lam, ab_w_out, c_w_in, c_conv_w, c_a_log, c_dt_bias, c_norm_g, c_w_out):
    b, s, d = x.shape
    n = b * s
    for i in range(DEPTH):
        j = i // 2
        x = _ffn(x.reshape(n, d), ffn1_wg[i].astype(BF16), ffn1_wu[i].astype(BF16), ffn1_wd[i].astype(BF16),
                 _row(ln_g[i, 0]), _row(ln_b[i, 0])).reshape(b, s, d)
        if i % 2 == 0:
            w = ab_w_in[j]
            o1, o2, o3 = A_WIDTH, A_WIDTH + A_KV_WIDTH, A_WIDTH + 2 * A_KV_WIDTH
            hd = A_HEAD_DIM
            swap = lambda m: jnp.concatenate([m[:, hd:2 * hd], m[:, 0:hd]], axis=1)
            win = jnp.concatenate([w[:, :o1], w[:, o1:o2], swap(w[:, o1:o2]), w[:, o2:o3], swap(w[:, o2:o3]),
                                   w[:, o3:]], axis=1).astype(BF16)
            x = _mixer_ab(x, win, a_sinks[j].astype(F32), b_conv_w[j], _row(b_conv_b[j]),
                          _block_diag(b_wa[j]).astype(BF16), _row(b_ba[j]),
                          _block_diag(b_wx[j]).astype(BF16), _row(b_bx[j]), _row(b_lam[j]),
                          ab_w_out[j].astype(BF16), _row(ln_g[i, 1]), _row(ln_b[i, 1]))
        else:
            w = c_w_in[j]
            w1 = w[:, :4 * C_WIDTH].astype(BF16)
            w2 = jnp.pad(w[:, 4 * C_WIDTH:], ((0, 0), (0, LANES - 2 * C_HEADS))).astype(BF16)
            pad_gate = lambda v: jnp.pad(v.astype(F32), (C_GATE_COL, LANES - C_GATE_COL - C_HEADS)).reshape(1, LANES)
            x = _mixer_c(x, w1, w2, c_conv_w[j], pad_gate(c_a_log[j]), pad_gate(c_dt_bias[j]),
                         _row(c_norm_g[j]), c_w_out[j].astype(BF16), _row(ln_g[i, 1]), _row(ln_b[i, 1]))
        x = _ffn(x.reshape(n, d), ffn2_wg[i].astype(BF16), ffn2_wu[i].astype(BF16), ffn2_wd[i].astype(BF16),
                 _row(ln_g[i, 2]), _row(ln_b[i, 2]),
                 ple_args=(p.reshape(DEPTH, n, -1), i, ple_wg[i].astype(BF16), _row(ple_bg[i]),
                           ple_wp[i].astype(BF16))).reshape(b, s, d)
    return x
```

```python
import functools

import jax
import jax.numpy as jnp
from jax import lax
from jax.experimental import pallas as pl
from jax.experimental.pallas import tpu as pltpu

F32 = jnp.float32
BF16 = jnp.bfloat16

DEPTH = 2
CHUNK = 64
A_HEADS = 8
A_KV_HEADS = 2
A_HEAD_DIM = 64
A_WIDTH = A_HEADS * A_HEAD_DIM
A_KV_WIDTH = A_KV_HEADS * A_HEAD_DIM
A_PREV_CHUNKS = 2
A_BAND = (A_PREV_CHUNKS + 1) * CHUNK
B_WIDTH = 512
B_BLOCKS = 8
B_CONV = 4
RG_C = 8.0
C_HEADS = 8
C_HEAD_DIM = 128
C_WIDTH = C_HEADS * C_HEAD_DIM
C_CONV = 4
DN_ALPHA = (2.0 * DEPTH) ** 0.25
LN_EPS = 1e-5
NORM_EPS = 1e-6
NEG = -1e30

LANES = 128
SUBLANES = 8
FFN_ROWS = 1024
FFN_COLS = 256
FFN_OUT_ROWS = 256
MIX_BATCH = 8
C_GROUP = 8
A_GROUP_SEQ = 8
CONV_COLS = 512
VMEM_LIMIT = 56 * 1024 * 1024


def _dot(a, b):
    return jnp.dot(a, b, preferred_element_type=F32)


def _dot_nt(a, b):
    return lax.dot_general(a, b, (((1,), (1,)), ((), ())), preferred_element_type=F32)


def _dot_tn(a, b):
    return lax.dot_general(a, b, (((0,), (0,)), ((), ())), preferred_element_type=F32)


def _layer_norm(z, g, b):
    mu = jnp.mean(z, axis=-1, keepdims=True)
    d = z - mu
    var = jnp.mean(d * d, axis=-1, keepdims=True)
    return d * lax.rsqrt(var + LN_EPS) * g + b


def _silu(x):
    return x * jax.nn.sigmoid(x)


def _softplus(x):
    e = jnp.exp(-jnp.abs(x))
    u = 1.0 + e
    log1p_e = jnp.where(u == 1.0, e, jnp.log(u) * (e / (u - 1.0)))
    return jnp.maximum(x, 0.0) + log1p_e


def _neg_expm1(y):
    return -jnp.tanh(0.5 * y) * (jnp.exp(y) + 1.0)


def _causal_conv(cur, tail, w):
    taps = w.shape[0]
    full = jnp.concatenate([tail, cur], axis=1)
    out = w[taps - 1:taps].reshape(1, 1, -1) * cur
    for d in range(1, taps):
        shifted = pltpu.roll(full, d, 1)[:, SUBLANES:, :]
        out = out + w[taps - 1 - d:taps - d].reshape(1, 1, -1) * shifted
    return out


def _resident(shape):
    zeros = (0,) * len(shape)
    return pl.BlockSpec(shape, lambda *_: zeros, pipeline_mode=pl.Buffered(1))


def _ffn_body(x_ref, wg_ref, wu_ref, wd_ref, lng_ref, lnb_ref, *rest, ple, n_chunks):
    if ple:
        p_ref, pwg_ref, pbg_ref, pwp_ref, o_ref, h_scr = rest
    else:
        o_ref, h_scr = rest
    xb = x_ref[...].astype(BF16)
    for c in range(n_chunks):
        sl = slice(c * FFN_COLS, (c + 1) * FFN_COLS)
        g = _dot(xb, wg_ref[:, sl])
        u = _dot(xb, wu_ref[:, sl])
        h_scr[:, sl] = (_silu(g) * u).astype(BF16)
    for r in range(x_ref.shape[0] // FFN_OUT_ROWS):
        rs = slice(r * FFN_OUT_ROWS, (r + 1) * FFN_OUT_ROWS)
        y = _dot(h_scr[rs, :], wd_ref[...])
        o_ref[rs, :] = _layer_norm(DN_ALPHA * x_ref[rs, :] + 0.5 * y, lng_ref[...], lnb_ref[...])
    if ple:
        out = o_ref[...]
        gate = jax.nn.sigmoid(_dot(out.astype(BF16), pwg_ref[...]) + pbg_ref[...])
        o_ref[...] = out + gate * _dot(p_ref[...].astype(BF16), pwp_ref[...])


def _ffn(x2, wg, wu, wd, lng, lnb, ple_args=None):
    n, d = x2.shape
    f = wg.shape[1]
    rows = min(FFN_ROWS, n)
    assert n % rows == 0 and f % FFN_COLS == 0 and rows % FFN_OUT_ROWS == 0
    row_spec = pl.BlockSpec((rows, d), lambda i: (i, 0))
    in_specs = [row_spec, _resident((d, f)), _resident((d, f)), _resident((f, d)),
                _resident((1, d)), _resident((1, d))]
    args = [x2, wg, wu, wd, lng, lnb]
    if ple_args is not None:
        p_all, layer, pwg, pbg, pwp = ple_args
        dp = p_all.shape[-1]
        in_specs += [pl.BlockSpec((None, rows, dp), lambda i: (layer, i, 0)), _resident((d, d)),
                     _resident((1, d)), _resident((dp, d))]
        args += [p_all, pwg, pbg, pwp]
    return pl.pallas_call(
        functools.partial(_ffn_body, ple=ple_args is not None, n_chunks=f // FFN_COLS),
        out_shape=jax.ShapeDtypeStruct((n, d), F32),
        grid=(n // rows,),
        in_specs=in_specs,
        out_specs=row_spec,
        scratch_shapes=[pltpu.VMEM((rows, f), BF16)],
        compiler_params=pltpu.CompilerParams(dimension_semantics=("arbitrary",),
                                             vmem_limit_bytes=VMEM_LIMIT),
        name="ffn_ple" if ple_args is not None else "ffn",
    )(*args)


AB_Q0 = 0
AB_K0 = A_WIDTH
AB_V0 = AB_K0 + 2 * A_KV_WIDTH
AB_X0 = AB_V0 + 2 * A_KV_WIDTH
AB_G0 = AB_X0 + B_WIDTH
AB_COLS = AB_G0 + B_WIDTH


def _mixer_ab_body(sinks_ref, x_ref, win_ref, convw_ref, convb_ref, wa_ref, ba_ref, wx_ref, bx_ref,
                   lam_ref, wout_ref, lng_ref, lnb_ref, o_ref,
                   proj_scr, kbuf, vbuf, ctail, hcarry, y_scr, *, bb, group):
    n = pl.program_id(1)
    rows = bb * CHUNK
    d_model = x_ref.shape[-1]

    @pl.when(n == 0)
    def _():
        kbuf[...] = jnp.zeros_like(kbuf)
        vbuf[...] = jnp.zeros_like(vbuf)
        ctail[...] = jnp.zeros_like(ctail)
        hcarry[...] = jnp.zeros_like(hcarry)

    x = x_ref[...].reshape(rows, d_model)
    proj_scr[...] = _dot(x.astype(BF16), win_ref[...])

    lo = lax.broadcasted_iota(jnp.int32, (rows, LANES), 1) < A_HEAD_DIM

    def masked_variants(c0):
        ta = proj_scr[:, c0:c0 + LANES]
        tb = proj_scr[:, c0 + LANES:c0 + 2 * LANES]
        zero = jnp.zeros_like(ta)
        parts = [jnp.where(lo, ta, zero), jnp.where(lo, zero, tb),
                 jnp.where(lo, tb, zero), jnp.where(lo, zero, ta)]
        return jnp.concatenate(parts, axis=1).astype(BF16).reshape(bb, CHUNK, 4 * LANES)

    old_k = kbuf[:, CHUNK:A_BAND, :]
    kbuf[:, 0:A_BAND - CHUNK, :] = old_k
    kbuf[:, A_BAND - CHUNK:A_BAND, :] = masked_variants(AB_K0)
    old_v = vbuf[:, CHUNK:A_BAND, :]
    vbuf[:, 0:A_BAND - CHUNK, :] = old_v
    vbuf[:, A_BAND - CHUNK:A_BAND, :] = masked_variants(AB_V0)

    qi = lax.broadcasted_iota(jnp.int32, (2 * CHUNK, A_BAND), 0)
    ki = lax.broadcasted_iota(jnp.int32, (2 * CHUNK, A_BAND), 1)
    first_tile = qi < CHUNK
    dist = jnp.abs(jnp.where(first_tile, qi, qi - CHUNK) + (A_BAND - CHUNK) - ki).astype(F32)
    valid = ki >= (A_BAND - CHUNK) - CHUNK * n
    row_first = lax.broadcasted_iota(jnp.int32, (2 * CHUNK, 1), 0) < CHUNK
    scale = A_HEAD_DIM ** -0.5

    def head_consts(kv, parity):
        ha = 4 * kv + parity
        hb = ha + 2
        slope = jnp.where(row_first, 2.0 ** (-8.0 * (ha + 1) / A_HEADS), 2.0 ** (-8.0 * (hb + 1) / A_HEADS))
        sink = jnp.where(row_first, sinks_ref[ha], sinks_ref[hb])
        return slope * dist, sink

    consts = [[head_consts(kv, par) for par in range(2)] for kv in range(A_KV_HEADS)]

    def attend(it, carry):
        combos = [(it * group + sq, kv, par) for sq in range(group) for kv in range(A_KV_HEADS) for par in range(2)]
        idx = range(len(combos))
        rsl = [pl.ds(pl.multiple_of(b * CHUNK, CHUNK), CHUNK) for b, _, _ in combos]
        tile = [slice((2 * kv + par) * LANES, (2 * kv + par + 1) * LANES) for _, kv, par in combos]
        bias = [consts[kv][par][0] for _, kv, par in combos]
        sink = [consts[kv][par][1] for _, kv, par in combos]
        q2 = {}
        for i, (b, kv, par) in enumerate(combos):
            if par == 0:
                c0 = AB_Q0 + 2 * LANES * kv
                qq = jnp.concatenate([proj_scr[rsl[i], c0:c0 + LANES], proj_scr[rsl[i], c0 + LANES:c0 + 2 * LANES]],
                                     axis=0)
                q2[i] = q2[i + 1] = (qq * scale).astype(BF16)
        s = [_dot_nt(q2[i], kbuf[b, :, tile[i]]) for i, (b, _, _) in enumerate(combos)]
        s = [jnp.where(valid, s[i] - bias[i], NEG) for i in idx]
        m = [jnp.maximum(jnp.max(s[i], axis=-1, keepdims=True), sink[i]) for i in idx]
        pr = [jnp.exp(s[i] - m[i]) for i in idx]
        den = [jnp.sum(pr[i], axis=-1, keepdims=True) + jnp.exp(sink[i] - m[i]) for i in idx]
        pv = [_dot(pr[i].astype(BF16), vbuf[b, :, tile[i]]) for i, (b, _, _) in enumerate(combos)]
        for i, (b, kv, par) in enumerate(combos):
            if par == 0:
                c0 = AB_Q0 + 2 * LANES * kv
                acc = pv[i] / den[i] + pv[i + 1] / den[i + 1]
                y_scr[rsl[i], c0:c0 + LANES] = acc[0:CHUNK]
                y_scr[rsl[i], c0 + LANES:c0 + 2 * LANES] = acc[CHUNK:2 * CHUNK]
        return carry

    lax.fori_loop(0, bb // group, attend, 0)

    bx3 = proj_scr[:, AB_X0:AB_X0 + B_WIDTH].reshape(bb, CHUNK, B_WIDTH)
    conv = _causal_conv(bx3, ctail[...], convw_ref[...]) + convb_ref[...].reshape(1, 1, B_WIDTH)
    ctail[...] = bx3[:, CHUNK - SUBLANES:CHUNK, :]
    c2 = conv.reshape(rows, B_WIDTH)
    cb = c2.astype(BF16)
    r = jax.nn.sigmoid(_dot(cb, wa_ref[...]) + ba_ref[...])
    i = jax.nn.sigmoid(_dot(cb, wx_ref[...]) + bx_ref[...])
    log_a = (-RG_C) * r * _softplus(-lam_ref[...])
    a = jnp.exp(log_a).reshape(bb, CHUNK, B_WIDTH)
    u = (jnp.sqrt(_neg_expm1(2.0 * log_a)) * (i * c2)).reshape(bb, CHUNK, B_WIDTH)
    ti = lax.broadcasted_iota(jnp.int32, (bb, CHUNK, B_WIDTH), 1)
    step = 1
    while step < SUBLANES:
        keep = ti >= step
        a_prev = jnp.where(keep, pltpu.roll(a, step, 1), 1.0)
        u_prev = jnp.where(keep, pltpu.roll(u, step, 1), 0.0)
        u = u + a * u_prev
        a = a * a_prev
        step *= 2
    while step < CHUNK:
        u_hi = u[:, step:, :] + a[:, step:, :] * u[:, :CHUNK - step, :]
        a_hi = a[:, step:, :] * a[:, :CHUNK - step, :]
        u = jnp.concatenate([u[:, :step, :], u_hi], axis=1)
        a = jnp.concatenate([a[:, :step, :], a_hi], axis=1)
        step *= 2
    h = u + a * hcarry[...].reshape(bb, 1, B_WIDTH)
    hcarry[...] = h[:, CHUNK - 1, :]
    gate = jax.nn.gelu(proj_scr[:, AB_G0:AB_G0 + B_WIDTH], approximate=True)
    y_scr[:, A_WIDTH:A_WIDTH + B_WIDTH] = h.reshape(rows, B_WIDTH) * gate

    y = _dot(y_scr[...].astype(BF16), wout_ref[...])
    out = _layer_norm(DN_ALPHA * x + y, lng_ref[...], lnb_ref[...])
    o_ref[...] = out.reshape(bb, CHUNK, d_model)


def _mixer_ab(x3, win, sinks, convw, convb, wa, ba, wx, bx, lam, wout, lng, lnb):
    b, s, d = x3.shape
    bb = min(MIX_BATCH, b)
    group = min(A_GROUP_SEQ, bb)
    assert b % bb == 0 and s % CHUNK == 0 and bb % group == 0
    rows = bb * CHUNK
    tile = pl.BlockSpec((bb, CHUNK, d), lambda i, j: (i, j, 0))
    in_specs = [pl.BlockSpec(memory_space=pltpu.SMEM), tile, _resident(win.shape),
                _resident(convw.shape), _resident(convb.shape), _resident(wa.shape), _resident(ba.shape),
                _resident(wx.shape), _resident(bx.shape), _resident(lam.shape), _resident(wout.shape),
                _resident(lng.shape), _resident(lnb.shape)]
    return pl.pallas_call(
        functools.partial(_mixer_ab_body, bb=bb, group=group),
        out_shape=jax.ShapeDtypeStruct((b, s, d), F32),
        grid=(b // bb, s // CHUNK),
        in_specs=in_specs,
        out_specs=tile,
        scratch_shapes=[
            pltpu.VMEM((rows, AB_COLS), F32),
            pltpu.VMEM((bb, A_BAND, 4 * LANES), BF16),
            pltpu.VMEM((bb, A_BAND, 4 * LANES), BF16),
            pltpu.VMEM((bb, SUBLANES, B_WIDTH), F32),
            pltpu.VMEM((bb, B_WIDTH), F32),
            pltpu.VMEM((rows, A_WIDTH + B_WIDTH), F32),
        ],
        compiler_params=pltpu.CompilerParams(dimension_semantics=("arbitrary", "arbitrary"),
                                             vmem_limit_bytes=VMEM_LIMIT),
        name="mixer_ab",
    )(sinks, x3, win, convw, convb, wa, ba, wx, bx, lam, wout, lng, lnb)


C_GATE_COL = C_HEADS


def _mixer_c_body(x_ref, w1_ref, w2_ref, convw_ref, alog_ref, dtb_ref, ng_ref, wout_ref, lng_ref, lnb_ref,
                  o_ref, ctail, qkv_scr, z_scr, ba_scr, gcol_scr, grow_scr, state, og_scr, *, bb, group):
    n = pl.program_id(1)
    rows = bb * CHUNK
    d_model = x_ref.shape[-1]
    qkv_w = 3 * C_WIDTH

    @pl.when(n == 0)
    def _():
        ctail[...] = jnp.zeros_like(ctail)
        state[...] = jnp.zeros_like(state)

    x = x_ref[...].reshape(rows, d_model)
    xb = x.astype(BF16)
    for c in range(qkv_w // CONV_COLS):
        csl = slice(c * CONV_COLS, (c + 1) * CONV_COLS)
        cur = _dot(xb, w1_ref[:, csl]).reshape(bb, CHUNK, CONV_COLS)
        conv = _causal_conv(cur, ctail[:, :, csl], convw_ref[:, csl])
        ctail[:, :, csl] = cur[:, CHUNK - SUBLANES:CHUNK, :]
        qkv_scr[:, csl] = _silu(conv).reshape(rows, CONV_COLS)
    z_scr[...] = _dot(xb, w1_ref[:, qkv_w:qkv_w + C_WIDTH])
    ba_scr[...] = _dot(xb, w2_ref[...])

    ri = lax.broadcasted_iota(jnp.int32, (CHUNK, CHUNK), 0)
    ci = lax.broadcasted_iota(jnp.int32, (CHUNK, CHUNK), 1)
    tril = ri >= ci
    ri2 = lax.broadcasted_iota(jnp.int32, (CHUNK, LANES), 0)
    li2 = lax.broadcasted_iota(jnp.int32, (CHUNK, LANES), 1)
    ci2 = jnp.bitwise_and(li2, CHUNK - 1)
    lane_lo = li2 < CHUNK
    tril2 = ri2 >= ci2
    strict2 = ri2 > ci2
    eye2 = (ri2 == ci2).astype(F32)
    keep_lo = jnp.where(lane_lo, 1.0, 0.0).astype(BF16)
    keep_hi = jnp.where(lane_lo, 0.0, 1.0).astype(BF16)
    zero_t = jnp.zeros((CHUNK, C_HEAD_DIM), BF16)
    neg_rate = -jnp.exp(alog_ref[...])
    dtb = dtb_ref[...]
    norm_g = ng_ref[...]
    qscale = C_HEAD_DIM ** -0.5

    bab = ba_scr[...]
    g_all = neg_rate * _softplus(bab + dtb)
    ba_scr[...] = jax.nn.sigmoid(bab)
    g_hi = g_all.astype(BF16)
    rem = g_all - g_hi.astype(F32)
    g_mid = rem.astype(BF16)
    g_lo = (rem - g_mid.astype(F32)).astype(BF16)
    pieces = jnp.concatenate([part[b * CHUNK:(b + 1) * CHUNK] for part in (g_hi, g_mid, g_lo) for b in range(bb)],
                             axis=1)
    csum = _dot(tril.astype(BF16), pieces)
    for b in range(bb):
        gcb = (csum[:, b * LANES:(b + 1) * LANES] + csum[:, (bb + b) * LANES:(bb + b + 1) * LANES]
               + csum[:, (2 * bb + b) * LANES:(2 * bb + b + 1) * LANES])
        gcol_scr[b * CHUNK:(b + 1) * CHUNK, :] = gcb
        grow_scr[b * LANES:(b + 1) * LANES, :] = jnp.concatenate([gcb, gcb], axis=0).T

    def unit_norm(t, extra):
        return t * (lax.rsqrt(jnp.sum(t * t, axis=-1, keepdims=True) + NORM_EPS) * extra)

    def head_cols(part, h):
        return slice(part * C_WIDTH + h * C_HEAD_DIM, part * C_WIDTH + (h + 1) * C_HEAD_DIM)

    def lane_cat(a, b):
        return jnp.concatenate([a, b], axis=1)

    def block_diag2(top, bottom):
        return jnp.concatenate([lane_cat(top, jnp.zeros_like(top)), lane_cat(jnp.zeros_like(bottom), bottom)], axis=0)

    def unit(it, carry):
        units = [(it * group + s, p) for s in range(group) for p in range(C_HEADS // 2)]
        idx = range(len(units))
        rsl = [pl.ds(pl.multiple_of(b * CHUNK, CHUNK), CHUNK) for b, _ in units]
        heads = [(2 * p, 2 * p + 1) for _, p in units]
        st = [[state[b * C_HEADS + h] for h in heads[i]] for i, (b, _) in enumerate(units)]
        beta = [[ba_scr[rsl[i], h:h + 1] for h in heads[i]] for i in idx]
        gc = [[gcol_scr[rsl[i], C_GATE_COL + h:C_GATE_COL + h + 1] for h in heads[i]] for i in idx]
        gr = [[grow_scr[pl.ds(pl.multiple_of(b * LANES, LANES) + C_GATE_COL + h, 1), :] for h in heads[i]]
              for i, (b, _) in enumerate(units)]
        gl = [[gc[i][j][CHUNK - 1:CHUNK, :] for j in range(2)] for i in idx]
        eg = [[jnp.exp(gc[i][j]) for j in range(2)] for i in idx]
        q = [[unit_norm(qkv_scr[rsl[i], head_cols(0, h)], qscale) for h in heads[i]] for i in idx]
        k = [[unit_norm(qkv_scr[rsl[i], head_cols(1, h)], 1.0) for h in heads[i]] for i in idx]
        v = [[qkv_scr[rsl[i], head_cols(2, h)] for h in heads[i]] for i in idx]
        kb = [[k[i][j] * beta[i][j] for j in range(2)] for i in idx]
        lhs = [jnp.concatenate([lane_cat(q[i][0], q[i][1]), lane_cat(kb[i][0], kb[i][1])], axis=0).astype(BF16)
               for i in idx]
        kbd = [block_diag2(k[i][0].astype(BF16), k[i][1].astype(BF16)) for i in idx]
        qk = [_dot_nt(lhs[i], kbd[i]) for i in idx]
        gc2 = [jnp.where(lane_lo, gc[i][0], gc[i][1]) for i in idx]
        gr2 = [jnp.where(lane_lo[0:1], gr[i][0], gr[i][1]) for i in idx]
        decay = [jnp.where(tril2, jnp.exp(jnp.where(tril2, gc2[i] - gr2[i], 0.0)), 0.0) for i in idx]
        attn = [(qk[i][0:CHUNK] * decay[i]).astype(BF16) for i in idx]
        lmat = [jnp.where(strict2, qk[i][CHUNK:2 * CHUNK] * decay[i], 0.0) for i in idx]
        tinv = [eye2 - lmat[i] for i in idx]
        pw = [lmat[i].astype(BF16) for i in idx]
        for _ in range(5):
            pw = [_dot(pw[i], jnp.concatenate([pw[i] * keep_lo, pw[i] * keep_hi], axis=0)).astype(BF16) for i in idx]
            tinv = [tinv[i] + _dot(tinv[i].astype(BF16), jnp.concatenate([pw[i] * keep_lo, pw[i] * keep_hi], axis=0))
                    for i in idx]
        rhs = [jnp.concatenate([
            jnp.concatenate([(v[i][0] * beta[i][0]).astype(BF16), zero_t, (kb[i][0] * eg[i][0]).astype(BF16), zero_t], axis=1),
            jnp.concatenate([zero_t, (v[i][1] * beta[i][1]).astype(BF16), zero_t, (kb[i][1] * eg[i][1]).astype(BF16)], axis=1),
        ], axis=0) for i in idx]
        uw = [_dot(tinv[i].astype(BF16), rhs[i]) for i in idx]
        sbd = [block_diag2(st[i][0].astype(BF16), st[i][1].astype(BF16)) for i in idx]
        wq = [_dot(jnp.concatenate([uw[i][:, 2 * C_HEAD_DIM:],
                                    lane_cat(q[i][0] * eg[i][0], q[i][1] * eg[i][1])], axis=0).astype(BF16), sbd[i])
              for i in idx]
        vnb = [(uw[i][:, 0:2 * C_HEAD_DIM] - wq[i][0:CHUNK]).astype(BF16) for i in idx]
        vbd = [block_diag2(vnb[i][:, 0:C_HEAD_DIM], vnb[i][:, C_HEAD_DIM:]) for i in idx]
        o = [wq[i][CHUNK:2 * CHUNK] + _dot(attn[i], vbd[i]) for i in idx]
        for i, (b, _) in enumerate(units):
            for j, h in enumerate(heads[i]):
                hs = slice(j * C_HEAD_DIM, (j + 1) * C_HEAD_DIM)
                kdec = (k[i][j] * jnp.exp(gl[i][j] - gc[i][j])).astype(BF16)
                state[b * C_HEADS + h] = st[i][j] * jnp.exp(gl[i][j]) + _dot_tn(kdec, vnb[i][:, hs])
                oh = o[i][:, hs]
                on = oh * lax.rsqrt(jnp.mean(oh * oh, axis=-1, keepdims=True) + NORM_EPS) * norm_g
                og_scr[rsl[i], head_cols(0, h)] = on * _silu(z_scr[rsl[i], head_cols(0, h)])
        return carry

    lax.fori_loop(0, bb // group, unit, 0)

    y = _dot(og_scr[...].astype(BF16), wout_ref[...])
    out = _layer_norm(DN_ALPHA * x + y, lng_ref[...], lnb_ref[...])
    o_ref[...] = out.reshape(bb, CHUNK, d_model)


def _mixer_c(x3, w1, w2, convw, alog, dtb, ng, wout, lng, lnb):
    b, s, d = x3.shape
    bb = min(MIX_BATCH, b)
    group = min(C_GROUP, bb)
    assert b % bb == 0 and s % CHUNK == 0 and bb % group == 0
    rows = bb * CHUNK
    tile = pl.BlockSpec((bb, CHUNK, d), lambda i, j: (i, j, 0))
    in_specs = [tile] + [_resident(a.shape) for a in (w1, w2, convw, alog, dtb, ng, wout, lng, lnb)]
    return pl.pallas_call(
        functools.partial(_mixer_c_body, bb=bb, group=group),
        out_shape=jax.ShapeDtypeStruct((b, s, d), F32),
        grid=(b // bb, s // CHUNK),
        in_specs=in_specs,
        out_specs=tile,
        scratch_shapes=[
            pltpu.VMEM((bb, SUBLANES, 3 * C_WIDTH), F32),
            pltpu.VMEM((rows, 3 * C_WIDTH), F32),
            pltpu.VMEM((rows, C_WIDTH), F32),
            pltpu.VMEM((rows, LANES), F32),
            pltpu.VMEM((rows, LANES), F32),
            pltpu.VMEM((bb * LANES, LANES), F32),
            pltpu.VMEM((bb * C_HEADS, C_HEAD_DIM, C_HEAD_DIM), F32),
            pltpu.VMEM((rows, C_WIDTH), F32),
        ],
        compiler_params=pltpu.CompilerParams(dimension_semantics=("arbitrary", "arbitrary"),
                                             vmem_limit_bytes=VMEM_LIMIT),
        name="mixer_c",
    )(x3, w1, w2, convw, alog, dtb, ng, wout, lng, lnb)


def _block_diag(w):
    nb, bs, _ = w.shape
    out = jnp.zeros((nb * bs, nb * bs), w.dtype)
    for i in range(nb):
        out = out.at[i * bs:(i + 1) * bs, i * bs:(i + 1) * bs].set(w[i])
    return out


def _row(v):
    return v.reshape(1, -1).astype(F32)


def kernel(x, p, ffn1_wg, ffn1_wu, ffn1_wd, ffn2_wg, ffn2_wu, ffn2_wd, ln_g, ln_b, ple_wg, ple_bg, ple_wp, ab_w_in, a_sinks, b_conv_w, b_conv_b, b_wa, b_ba, b_wx, b_bx, b_lam, ab_w_out, c_w_in, c_conv_w, c_a_log, c_dt_bias, c_norm_g, c_w_out):
    b, s, d = x.shape
    n = b * s
    for i in range(DEPTH):
        j = i // 2
        x = _ffn(x.reshape(n, d), ffn1_wg[i].astype(BF16), ffn1_wu[i].astype(BF16), ffn1_wd[i].astype(BF16),
                 _row(ln_g[i, 0]), _row(ln_b[i, 0])).reshape(b, s, d)
        if i % 2 == 0:
            w = ab_w_in[j]
            o1, o2, o3 = A_WIDTH, A_WIDTH + A_KV_WIDTH, A_WIDTH + 2 * A_KV_WIDTH
            hd = A_HEAD_DIM
            swap = lambda m: jnp.concatenate([m[:, hd:2 * hd], m[:, 0:hd]], axis=1)
            win = jnp.concatenate([w[:, :o1], w[:, o1:o2], swap(w[:, o1:o2]), w[:, o2:o3], swap(w[:, o2:o3]),
                                   w[:, o3:]], axis=1).astype(BF16)
            x = _mixer_ab(x, win, a_sinks[j].astype(F32), b_conv_w[j], _row(b_conv_b[j]),
                          _block_diag(b_wa[j]).astype(BF16), _row(b_ba[j]),
                          _block_diag(b_wx[j]).astype(BF16), _row(b_bx[j]), _row(b_lam[j]),
                          ab_w_out[j].astype(BF16), _row(ln_g[i, 1]), _row(ln_b[i, 1]))
        else:
            w = c_w_in[j]
            w1 = w[:, :4 * C_WIDTH].astype(BF16)
            w2 = jnp.pad(w[:, 4 * C_WIDTH:], ((0, 0), (0, LANES - 2 * C_HEADS))).astype(BF16)
            pad_gate = lambda v: jnp.pad(v.astype(F32), (C_GATE_COL, LANES - C_GATE_COL - C_HEADS)).reshape(1, LANES)
            x = _mixer_c(x, w1, w2, c_conv_w[j], pad_gate(c_a_log[j]), pad_gate(c_dt_bias[j]),
                         _row(c_norm_g[j]), c_w_out[j].astype(BF16), _row(ln_g[i, 1]), _row(ln_b[i, 1]))
        x = _ffn(x.reshape(n, d), ffn2_wg[i].astype(BF16), ffn2_wu[i].astype(BF16), ffn2_wd[i].astype(BF16),
                 _row(ln_g[i, 2]), _row(ln_b[i, 2]),
                 ple_args=(p.reshape(DEPTH, n, -1), i, ple_wg[i].astype(BF16), _row(ple_bg[i]),
                           ple_wp[i].astype(BF16))).reshape(b, s, d)
    return x
```

```python
import functools

import jax
import jax.numpy as jnp
from jax import lax
from jax.experimental import pallas as pl
from jax.experimental.pallas import tpu as pltpu

F32 = jnp.float32
BF16 = jnp.bfloat16

DEPTH = 2
CHUNK = 64
A_HEADS = 8
A_KV_HEADS = 2
A_HEAD_DIM = 64
A_WIDTH = A_HEADS * A_HEAD_DIM
A_KV_WIDTH = A_KV_HEADS * A_HEAD_DIM
A_PREV_CHUNKS = 2
A_BAND = (A_PREV_CHUNKS + 1) * CHUNK
B_WIDTH = 512
B_BLOCKS = 8
B_CONV = 4
RG_C = 8.0
C_HEADS = 8
C_HEAD_DIM = 128
C_WIDTH = C_HEADS * C_HEAD_DIM
C_CONV = 4
DN_ALPHA = (2.0 * DEPTH) ** 0.25
LN_EPS = 1e-5
NORM_EPS = 1e-6
NEG = -1e30

LANES = 128
SUBLANES = 8
FFN_ROWS = 1024
FFN_COLS = 256
FFN_OUT_ROWS = 256
MIX_BATCH = 8
C_GROUP = 8
A_GROUP_SEQ = 8
CONV_COLS = 512
VMEM_LIMIT = 56 * 1024 * 1024


def _dot(a, b):
    return jnp.dot(a, b, preferred_element_type=F32)


def _dot_nt(a, b):
    return lax.dot_general(a, b, (((1,), (1,)), ((), ())), preferred_element_type=F32)


def _dot_tn(a, b):
    return lax.dot_general(a, b, (((0,), (0,)), ((), ())), preferred_element_type=F32)


def _layer_norm(z, g, b):
    mu = jnp.mean(z, axis=-1, keepdims=True)
    d = z - mu
    var = jnp.mean(d * d, axis=-1, keepdims=True)
    return d * lax.rsqrt(var + LN_EPS) * g + b


def _silu(x):
    return x * jax.nn.sigmoid(x)


def _softplus(x):
    e = jnp.exp(-jnp.abs(x))
    u = 1.0 + e
    log1p_e = jnp.where(u == 1.0, e, jnp.log(u) * (e / (u - 1.0)))
    return jnp.maximum(x, 0.0) + log1p_e


def _neg_expm1(y):
    return -jnp.tanh(0.5 * y) * (jnp.exp(y) + 1.0)


def _causal_conv(cur, tail, w):
    taps = w.shape[0]
    b, t, c = cur.shape
    nt = t // SUBLANES
    tiles = jnp.concatenate([tail, cur], axis=1).reshape(b * (nt + 1), SUBLANES, c)
    sub = lax.broadcasted_iota(jnp.int32, (b, nt, SUBLANES, c), 2)
    out = w[taps - 1:taps].reshape(1, 1, -1) * cur
    for d in range(1, taps):
        rot = pltpu.roll(tiles, d, 1).reshape(b, nt + 1, SUBLANES, c)
        shifted = jnp.where(sub < d, rot[:, 0:nt], rot[:, 1:nt + 1]).reshape(b, t, c)
        out = out + w[taps - 1 - d:taps - d].reshape(1, 1, -1) * shifted
    return out


def _resident(shape):
    zeros = (0,) * len(shape)
    return pl.BlockSpec(shape, lambda *_: zeros, pipeline_mode=pl.Buffered(1))


def _ffn_body(x_ref, wg_ref, wu_ref, wd_ref, lng_ref, lnb_ref, *rest, ple, n_chunks):
    if ple:
        p_ref, pwg_ref, pbg_ref, pwp_ref, o_ref, h_scr = rest
    else:
        o_ref, h_scr = rest
    xb = x_ref[...].astype(BF16)
    for c in range(n_chunks):
        sl = slice(c * FFN_COLS, (c + 1) * FFN_COLS)
        g = _dot(xb, wg_ref[:, sl])
        u = _dot(xb, wu_ref[:, sl])
        h_scr[:, sl] = (_silu(g) * u).astype(BF16)
    for r in range(x_ref.shape[0] // FFN_OUT_ROWS):
        rs = slice(r * FFN_OUT_ROWS, (r + 1) * FFN_OUT_ROWS)
        y = _dot(h_scr[rs, :], wd_ref[...])
        o_ref[rs, :] = _layer_norm(DN_ALPHA * x_ref[rs, :] + 0.5 * y, lng_ref[...], lnb_ref[...])
    if ple:
        out = o_ref[...]
        gate = jax.nn.sigmoid(_dot(out.astype(BF16), pwg_ref[...]) + pbg_ref[...])
        o_ref[...] = out + gate * _dot(p_ref[...].astype(BF16), pwp_ref[...])


def _ffn(x2, layer, wg, wu, wd, lng, lnb, ple_args=None):
    n, d = x2.shape
    f = wg.shape[-1]
    rows = min(FFN_ROWS, n)
    assert n % rows == 0 and f % FFN_COLS == 0 and rows % FFN_OUT_ROWS == 0
    row_spec = pl.BlockSpec((rows, d), lambda i: (i, 0))

    def of_layer(shape):
        return pl.BlockSpec((None,) + shape, lambda i: (layer,) + (0,) * len(shape), pipeline_mode=pl.Buffered(1))

    in_specs = [row_spec, of_layer((d, f)), of_layer((d, f)), of_layer((f, d)),
                _resident((1, d)), _resident((1, d))]
    args = [x2, wg, wu, wd, lng, lnb]
    if ple_args is not None:
        p_all, pwg, pbg, pwp = ple_args
        dp = p_all.shape[-1]
        in_specs += [pl.BlockSpec((None, rows, dp), lambda i: (layer, i, 0)), of_layer((d, d)),
                     _resident((1, d)), of_layer((dp, d))]
        args += [p_all, pwg, pbg, pwp]
    return pl.pallas_call(
        functools.partial(_ffn_body, ple=ple_args is not None, n_chunks=f // FFN_COLS),
        out_shape=jax.ShapeDtypeStruct((n, d), F32),
        grid=(n // rows,),
        in_specs=in_specs,
        out_specs=row_spec,
        scratch_shapes=[pltpu.VMEM((rows, f), BF16)],
        compiler_params=pltpu.CompilerParams(dimension_semantics=("arbitrary",),
                                             vmem_limit_bytes=VMEM_LIMIT),
        name="ffn_ple" if ple_args is not None else "ffn",
    )(*args)


AB_Q0 = 0
AB_K0 = A_WIDTH
AB_V0 = AB_K0 + 2 * A_KV_WIDTH
AB_X0 = AB_V0 + 2 * A_KV_WIDTH
AB_G0 = AB_X0 + B_WIDTH
AB_COLS = AB_G0 + B_WIDTH


def _mixer_ab_body(sinks_ref, x_ref, win_ref, convw_ref, convb_ref, wa_ref, ba_ref, wx_ref, bx_ref,
                   lam_ref, wout_ref, lng_ref, lnb_ref, o_ref,
                   proj_scr, kbuf, vbuf, ctail, hcarry, y_scr, *, bb, group):
    n = pl.program_id(1)
    rows = bb * CHUNK
    d_model = x_ref.shape[-1]

    @pl.when(n == 0)
    def _():
        kbuf[...] = jnp.zeros_like(kbuf)
        vbuf[...] = jnp.zeros_like(vbuf)
        ctail[...] = jnp.zeros_like(ctail)
        hcarry[...] = jnp.zeros_like(hcarry)

    x = x_ref[...].reshape(rows, d_model)
    proj_scr[...] = _dot(x.astype(BF16), win_ref[...])

    lo = lax.broadcasted_iota(jnp.int32, (rows, LANES), 1) < A_HEAD_DIM

    def masked_variants(c0):
        ta = proj_scr[:, c0:c0 + LANES]
        tb = proj_scr[:, c0 + LANES:c0 + 2 * LANES]
        zero = jnp.zeros_like(ta)
        parts = [jnp.where(lo, ta, zero), jnp.where(lo, zero, tb),
                 jnp.where(lo, tb, zero), jnp.where(lo, zero, ta)]
        return jnp.concatenate(parts, axis=1).astype(BF16).reshape(bb, CHUNK, 4 * LANES)

    old_k = kbuf[:, CHUNK:A_BAND, :]
    kbuf[:, 0:A_BAND - CHUNK, :] = old_k
    kbuf[:, A_BAND - CHUNK:A_BAND, :] = masked_variants(AB_K0)
    old_v = vbuf[:, CHUNK:A_BAND, :]
    vbuf[:, 0:A_BAND - CHUNK, :] = old_v
    vbuf[:, A_BAND - CHUNK:A_BAND, :] = masked_variants(AB_V0)

    qi = lax.broadcasted_iota(jnp.int32, (2 * CHUNK, A_BAND), 0)
    ki = lax.broadcasted_iota(jnp.int32, (2 * CHUNK, A_BAND), 1)
    first_tile = qi < CHUNK
    dist = jnp.abs(jnp.where(first_tile, qi, qi - CHUNK) + (A_BAND - CHUNK) - ki).astype(F32)
    valid = ki >= (A_BAND - CHUNK) - CHUNK * n
    row_first = lax.broadcasted_iota(jnp.int32, (2 * CHUNK, 1), 0) < CHUNK
    scale = A_HEAD_DIM ** -0.5

    def head_consts(kv, parity):
        ha = 4 * kv + parity
        hb = ha + 2
        slope = jnp.where(row_first, 2.0 ** (-8.0 * (ha + 1) / A_HEADS), 2.0 ** (-8.0 * (hb + 1) / A_HEADS))
        sink = jnp.where(row_first, sinks_ref[ha], sinks_ref[hb])
        return slope * dist, sink

    consts = [[head_consts(kv, par) for par in range(2)] for kv in range(A_KV_HEADS)]

    def attend(it, carry):
        combos = [(it * group + sq, kv, par) for sq in range(group) for kv in range(A_KV_HEADS) for par in range(2)]
        idx = range(len(combos))
        rsl = [pl.ds(pl.multiple_of(b * CHUNK, CHUNK), CHUNK) for b, _, _ in combos]
        tile = [slice((2 * kv + par) * LANES, (2 * kv + par + 1) * LANES) for _, kv, par in combos]
        bias = [consts[kv][par][0] for _, kv, par in combos]
        sink = [consts[kv][par][1] for _, kv, par in combos]
        q2 = {}
        for i, (b, kv, par) in enumerate(combos):
            if par == 0:
                c0 = AB_Q0 + 2 * LANES * kv
                qq = jnp.concatenate([proj_scr[rsl[i], c0:c0 + LANES], proj_scr[rsl[i], c0 + LANES:c0 + 2 * LANES]],
                                     axis=0)
                q2[i] = q2[i + 1] = (qq * scale).astype(BF16)
        s = [_dot_nt(q2[i], kbuf[b, :, tile[i]]) for i, (b, _, _) in enumerate(combos)]
        s = [jnp.where(valid, s[i] - bias[i], NEG) for i in idx]
        m = [jnp.maximum(jnp.max(s[i], axis=-1, keepdims=True), sink[i]) for i in idx]
        pr = [jnp.exp(s[i] - m[i]) for i in idx]
        den = [jnp.sum(pr[i], axis=-1, keepdims=True) + jnp.exp(sink[i] - m[i]) for i in idx]
        pv = [_dot(pr[i].astype(BF16), vbuf[b, :, tile[i]]) for i, (b, _, _) in enumerate(combos)]
        for i, (b, kv, par) in enumerate(combos):
            if par == 0:
                c0 = AB_Q0 + 2 * LANES * kv
                acc = pv[i] / den[i] + pv[i + 1] / den[i + 1]
                y_scr[rsl[i], c0:c0 + LANES] = acc[0:CHUNK]
                y_scr[rsl[i], c0 + LANES:c0 + 2 * LANES] = acc[CHUNK:2 * CHUNK]
        return carry

    lax.fori_loop(0, bb // group, attend, 0)

    bx3 = proj_scr[:, AB_X0:AB_X0 + B_WIDTH].reshape(bb, CHUNK, B_WIDTH)
    conv = _causal_conv(bx3, ctail[...], convw_ref[...]) + convb_ref[...].reshape(1, 1, B_WIDTH)
    ctail[...] = bx3[:, CHUNK - SUBLANES:CHUNK, :]
    c2 = conv.reshape(rows, B_WIDTH)
    cb = c2.astype(BF16)
    r = jax.nn.sigmoid(_dot(cb, wa_ref[...]) + ba_ref[...])
    i = jax.nn.sigmoid(_dot(cb, wx_ref[...]) + bx_ref[...])
    log_a = (-RG_C) * r * _softplus(-lam_ref[...])
    n_tiles = CHUNK // SUBLANES
    a = jnp.exp(log_a).reshape(bb * n_tiles, SUBLANES, B_WIDTH)
    u = (jnp.sqrt(_neg_expm1(2.0 * log_a)) * (i * c2)).reshape(bb * n_tiles, SUBLANES, B_WIDTH)
    ti = lax.broadcasted_iota(jnp.int32, (bb * n_tiles, SUBLANES, B_WIDTH), 1)
    step = 1
    while step < SUBLANES:
        keep = ti >= step
        a_prev = jnp.where(keep, pltpu.roll(a, step, 1), 1.0)
        u_prev = jnp.where(keep, pltpu.roll(u, step, 1), 0.0)
        u = u + a * u_prev
        a = a * a_prev
        step *= 2
    a = a.reshape(bb, n_tiles, SUBLANES, B_WIDTH)
    u = u.reshape(bb, n_tiles, SUBLANES, B_WIDTH)
    carry = hcarry[...].reshape(bb, 1, B_WIDTH)
    h_tiles = []
    for t in range(n_tiles):
        h_t = u[:, t] + a[:, t] * carry
        carry = h_t[:, SUBLANES - 1:SUBLANES, :]
        h_tiles.append(h_t)
    hcarry[...] = carry.reshape(bb, B_WIDTH)
    h = jnp.concatenate(h_tiles, axis=1)
    gate = jax.nn.gelu(proj_scr[:, AB_G0:AB_G0 + B_WIDTH], approximate=True)
    y_scr[:, A_WIDTH:A_WIDTH + B_WIDTH] = h.reshape(rows, B_WIDTH) * gate

    y = _dot(y_scr[...].astype(BF16), wout_ref[...])
    out = _layer_norm(DN_ALPHA * x + y, lng_ref[...], lnb_ref[...])
    o_ref[...] = out.reshape(bb, CHUNK, d_model)


def _mixer_ab(x3, win, sinks, convw, convb, wa, ba, wx, bx, lam, wout, lng, lnb):
    b, s, d = x3.shape
    bb = min(MIX_BATCH, b)
    group = min(A_GROUP_SEQ, bb)
    assert b % bb == 0 and s % CHUNK == 0 and bb % group == 0
    rows = bb * CHUNK
    tile = pl.BlockSpec((bb, CHUNK, d), lambda i, j: (i, j, 0))
    in_specs = [pl.BlockSpec(memory_space=pltpu.SMEM), tile, _resident(win.shape),
                _resident(convw.shape), _resident(convb.shape), _resident(wa.shape), _resident(ba.shape),
                _resident(wx.shape), _resident(bx.shape), _resident(lam.shape), _resident(wout.shape),
                _resident(lng.shape), _resident(lnb.shape)]
    return pl.pallas_call(
        functools.partial(_mixer_ab_body, bb=bb, group=group),
        out_shape=jax.ShapeDtypeStruct((b, s, d), F32),
        grid=(b // bb, s // CHUNK),
        in_specs=in_specs,
        out_specs=tile,
        scratch_shapes=[
            pltpu.VMEM((rows, AB_COLS), F32),
            pltpu.VMEM((bb, A_BAND, 4 * LANES), BF16),
            pltpu.VMEM((bb, A_BAND, 4 * LANES), BF16),
            pltpu.VMEM((bb, SUBLANES, B_WIDTH), F32),
            pltpu.VMEM((bb, B_WIDTH), F32),
            pltpu.VMEM((rows, A_WIDTH + B_WIDTH), F32),
        ],
        compiler_params=pltpu.CompilerParams(dimension_semantics=("arbitrary", "arbitrary"),
                                             vmem_limit_bytes=VMEM_LIMIT),
        name="mixer_ab",
    )(sinks, x3, win, convw, convb, wa, ba, wx, bx, lam, wout, lng, lnb)


C_GATE_COL = C_HEADS


def _mixer_c_body(x_ref, w1_ref, w2_ref, convw_ref, alog_ref, dtb_ref, ng_ref, wout_ref, lng_ref, lnb_ref,
                  o_ref, ctail, qkv_scr, z_scr, ba_scr, gcol_scr, grow_scr, state, og_scr, *, bb, group):
    n = pl.program_id(1)
    rows = bb * CHUNK
    d_model = x_ref.shape[-1]
    qkv_w = 3 * C_WIDTH

    @pl.when(n == 0)
    def _():
        ctail[...] = jnp.zeros_like(ctail)
        state[...] = jnp.zeros_like(state)

    x = x_ref[...].reshape(rows, d_model)
    xb = x.astype(BF16)
    for c in range(qkv_w // CONV_COLS):
        csl = slice(c * CONV_COLS, (c + 1) * CONV_COLS)
        cur = _dot(xb, w1_ref[:, csl]).reshape(bb, CHUNK, CONV_COLS)
        conv = _causal_conv(cur, ctail[:, :, csl], convw_ref[:, csl])
        ctail[:, :, csl] = cur[:, CHUNK - SUBLANES:CHUNK, :]
        qkv_scr[:, csl] = _silu(conv).reshape(rows, CONV_COLS)
    z_scr[...] = _dot(xb, w1_ref[:, qkv_w:qkv_w + C_WIDTH])
    ba_scr[...] = _dot(xb, w2_ref[...])

    ri = lax.broadcasted_iota(jnp.int32, (CHUNK, CHUNK), 0)
    ci = lax.broadcasted_iota(jnp.int32, (CHUNK, CHUNK), 1)
    tril = ri >= ci
    ri2 = lax.broadcasted_iota(jnp.int32, (CHUNK, LANES), 0)
    li2 = lax.broadcasted_iota(jnp.int32, (CHUNK, LANES), 1)
    ci2 = jnp.bitwise_and(li2, CHUNK - 1)
    lane_lo = li2 < CHUNK
    tril2 = ri2 >= ci2
    strict2 = ri2 > ci2
    eye2 = (ri2 == ci2).astype(F32)
    keep_lo = jnp.where(lane_lo, 1.0, 0.0).astype(BF16)
    keep_hi = jnp.where(lane_lo, 0.0, 1.0).astype(BF16)
    zero_t = jnp.zeros((CHUNK, C_HEAD_DIM), BF16)
    neg_rate = -jnp.exp(alog_ref[...])
    dtb = dtb_ref[...]
    norm_g = ng_ref[...]
    qscale = C_HEAD_DIM ** -0.5

    bab = ba_scr[...]
    g_all = neg_rate * _softplus(bab + dtb)
    ba_scr[...] = jax.nn.sigmoid(bab)
    g_hi = g_all.astype(BF16)
    rem = g_all - g_hi.astype(F32)
    g_mid = rem.astype(BF16)
    g_lo = (rem - g_mid.astype(F32)).astype(BF16)
    pieces = jnp.concatenate([part[b * CHUNK:(b + 1) * CHUNK] for part in (g_hi, g_mid, g_lo) for b in range(bb)],
                             axis=1)
    csum = _dot(tril.astype(BF16), pieces)
    for b in range(bb):
        gcb = (csum[:, b * LANES:(b + 1) * LANES] + csum[:, (bb + b) * LANES:(bb + b + 1) * LANES]
               + csum[:, (2 * bb + b) * LANES:(2 * bb + b + 1) * LANES])
        gcol_scr[b * CHUNK:(b + 1) * CHUNK, :] = gcb
        grow_scr[b * LANES:(b + 1) * LANES, :] = jnp.concatenate([gcb, gcb], axis=0).T

    def unit_norm(t, extra):
        return t * (lax.rsqrt(jnp.sum(t * t, axis=-1, keepdims=True) + NORM_EPS) * extra)

    def head_cols(part, h):
        return slice(part * C_WIDTH + h * C_HEAD_DIM, part * C_WIDTH + (h + 1) * C_HEAD_DIM)

    def lane_cat(a, b):
        return jnp.concatenate([a, b], axis=1)

    def block_diag2(top, bottom):
        return jnp.concatenate([lane_cat(top, jnp.zeros_like(top)), lane_cat(jnp.zeros_like(bottom), bottom)], axis=0)

    def unit(it, carry):
        units = [(it * group + s, p) for s in range(group) for p in range(C_HEADS // 2)]
        idx = range(len(units))
        rsl = [pl.ds(pl.multiple_of(b * CHUNK, CHUNK), CHUNK) for b, _ in units]
        heads = [(2 * p, 2 * p + 1) for _, p in units]
        st = [[state[b * C_HEADS + h] for h in heads[i]] for i, (b, _) in enumerate(units)]
        beta = [[ba_scr[rsl[i], h:h + 1] for h in heads[i]] for i in idx]
        gc = [[gcol_scr[rsl[i], C_GATE_COL + h:C_GATE_COL + h + 1] for h in heads[i]] for i in idx]
        gr = [[grow_scr[pl.ds(pl.multiple_of(b * LANES, LANES) + C_GATE_COL + h, 1), :] for h in heads[i]]
              for i, (b, _) in enumerate(units)]
        gl = [[gc[i][j][CHUNK - 1:CHUNK, :] for j in range(2)] for i in idx]
        eg = [[jnp.exp(gc[i][j]) for j in range(2)] for i in idx]
        q = [[unit_norm(qkv_scr[rsl[i], head_cols(0, h)], qscale) for h in heads[i]] for i in idx]
        k = [[unit_norm(qkv_scr[rsl[i], head_cols(1, h)], 1.0) for h in heads[i]] for i in idx]
        v = [[qkv_scr[rsl[i], head_cols(2, h)] for h in heads[i]] for i in idx]
        kb = [[k[i][j] * beta[i][j] for j in range(2)] for i in idx]
        lhs = [jnp.concatenate([lane_cat(q[i][0], q[i][1]), lane_cat(kb[i][0], kb[i][1])], axis=0).astype(BF16)
               for i in idx]
        kbd = [block_diag2(k[i][0].astype(BF16), k[i][1].astype(BF16)) for i in idx]
        qk = [_dot_nt(lhs[i], kbd[i]) for i in idx]
        gc2 = [jnp.where(lane_lo, gc[i][0], gc[i][1]) for i in idx]
        gr2 = [jnp.where(lane_lo[0:1], gr[i][0], gr[i][1]) for i in idx]
        decay = [jnp.where(tril2, jnp.exp(jnp.where(tril2, gc2[i] - gr2[i], 0.0)), 0.0) for i in idx]
        attn = [(qk[i][0:CHUNK] * decay[i]).astype(BF16) for i in idx]
        lmat = [jnp.where(strict2, qk[i][CHUNK:2 * CHUNK] * decay[i], 0.0) for i in idx]
        tinv = [eye2 - lmat[i] for i in idx]
        pw = [lmat[i].astype(BF16) for i in idx]
        for _ in range(5):
            pw = [_dot(pw[i], jnp.concatenate([pw[i] * keep_lo, pw[i] * keep_hi], axis=0)).astype(BF16) for i in idx]
            tinv = [tinv[i] + _dot(tinv[i].astype(BF16), jnp.concatenate([pw[i] * keep_lo, pw[i] * keep_hi], axis=0))
                    for i in idx]
        rhs = [jnp.concatenate([
            jnp.concatenate([(v[i][0] * beta[i][0]).astype(BF16), zero_t, (kb[i][0] * eg[i][0]).astype(BF16), zero_t], axis=1),
            jnp.concatenate([zero_t, (v[i][1] * beta[i][1]).astype(BF16), zero_t, (kb[i][1] * eg[i][1]).astype(BF16)], axis=1),
        ], axis=0) for i in idx]
        uw = [_dot(tinv[i].astype(BF16), rhs[i]) for i in idx]
        sbd = [block_diag2(st[i][0].astype(BF16), st[i][1].astype(BF16)) for i in idx]
        wq = [_dot(jnp.concatenate([uw[i][:, 2 * C_HEAD_DIM:],
                                    lane_cat(q[i][0] * eg[i][0], q[i][1] * eg[i][1])], axis=0).astype(BF16), sbd[i])
              for i in idx]
        vnb = [(uw[i][:, 0:2 * C_HEAD_DIM] - wq[i][0:CHUNK]).astype(BF16) for i in idx]
        vbd = [block_diag2(vnb[i][:, 0:C_HEAD_DIM], vnb[i][:, C_HEAD_DIM:]) for i in idx]
        o = [wq[i][CHUNK:2 * CHUNK] + _dot(attn[i], vbd[i]) for i in idx]
        for i, (b, _) in enumerate(units):
            for j, h in enumerate(heads[i]):
                hs = slice(j * C_HEAD_DIM, (j + 1) * C_HEAD_DIM)
                kdec = (k[i][j] * jnp.exp(gl[i][j] - gc[i][j])).astype(BF16)
                state[b * C_HEADS + h] = st[i][j] * jnp.exp(gl[i][j]) + _dot_tn(kdec, vnb[i][:, hs])
                oh = o[i][:, hs]
                on = oh * lax.rsqrt(jnp.mean(oh * oh, axis=-1, keepdims=True) + NORM_EPS) * norm_g
                og_scr[rsl[i], head_cols(0, h)] = on * _silu(z_scr[rsl[i], head_cols(0, h)])
        return carry

    lax.fori_loop(0, bb // group, unit, 0)

    y = _dot(og_scr[...].astype(BF16), wout_ref[...])
    out = _layer_norm(DN_ALPHA * x + y, lng_ref[...], lnb_ref[...])
    o_ref[...] = out.reshape(bb, CHUNK, d_model)


def _mixer_c(x3, w1, w2, convw, alog, dtb, ng, wout, lng, lnb):
    b, s, d = x3.shape
    bb = min(MIX_BATCH, b)
    group = min(C_GROUP, bb)
    assert b % bb == 0 and s % CHUNK == 0 and bb % group == 0
    rows = bb * CHUNK
    tile = pl.BlockSpec((bb, CHUNK, d), lambda i, j: (i, j, 0))
    in_specs = [tile] + [_resident(a.shape) for a in (w1, w2, convw, alog, dtb, ng, wout, lng, lnb)]
    return pl.pallas_call(
        functools.partial(_mixer_c_body, bb=bb, group=group),
        out_shape=jax.ShapeDtypeStruct((b, s, d), F32),
        grid=(b // bb, s // CHUNK),
        in_specs=in_specs,
        out_specs=tile,
        scratch_shapes=[
            pltpu.VMEM((bb, SUBLANES, 3 * C_WIDTH), F32),
            pltpu.VMEM((rows, 3 * C_WIDTH), F32),
            pltpu.VMEM((rows, C_WIDTH), F32),
            pltpu.VMEM((rows, LANES), F32),
            pltpu.VMEM((rows, LANES), F32),
            pltpu.VMEM((bb * LANES, LANES), F32),
            pltpu.VMEM((bb * C_HEADS, C_HEAD_DIM, C_HEAD_DIM), F32),
            pltpu.VMEM((rows, C_WIDTH), F32),
        ],
        compiler_params=pltpu.CompilerParams(dimension_semantics=("arbitrary", "arbitrary"),
                                             vmem_limit_bytes=VMEM_LIMIT),
        name="mixer_c",
    )(x3, w1, w2, convw, alog, dtb, ng, wout, lng, lnb)


def _block_diag(w):
    nb, bs, _ = w.shape
    out = jnp.zeros((nb * bs, nb * bs), w.dtype)
    for i in range(nb):
        out = out.at[i * bs:(i + 1) * bs, i * bs:(i + 1) * bs].set(w[i])
    return out


def _row(v):
    return v.reshape(1, -1).astype(F32)


def kernel(x, p, ffn1_wg, ffn1_wu, ffn1_wd, ffn2_wg, ffn2_wu, ffn2_wd, ln_g, ln_b, ple_wg, ple_bg, ple_wp, ab_w_in, a_sinks, b_conv_w, b_conv_b, b_wa, b_ba, b_wx, b_bx, b_lam, ab_w_out, c_w_in, c_conv_w, c_a_log, c_dt_bias, c_norm_g, c_w_out):
    b, s, d = x.shape
    n = b * s
    ffn1 = [w.astype(BF16) for w in (ffn1_wg, ffn1_wu, ffn1_wd)]
    ffn2 = [w.astype(BF16) for w in (ffn2_wg, ffn2_wu, ffn2_wd)]
    ple_w = (ple_wg.astype(BF16), ple_wp.astype(BF16))
    p_all = p.reshape(DEPTH, n, -1)
    for i in range(DEPTH):
        j = i // 2
        x = _ffn(x.reshape(n, d), i, *ffn1, _row(ln_g[i, 0]), _row(ln_b[i, 0])).reshape(b, s, d)
        if i % 2 == 0:
            w = ab_w_in[j]
            o1, o2, o3 = A_WIDTH, A_WIDTH + A_KV_WIDTH, A_WIDTH + 2 * A_KV_WIDTH
            hd = A_HEAD_DIM
            swap = lambda m: jnp.concatenate([m[:, hd:2 * hd], m[:, 0:hd]], axis=1)
            win = jnp.concatenate([w[:, :o1], w[:, o1:o2], swap(w[:, o1:o2]), w[:, o2:o3], swap(w[:, o2:o3]),
                                   w[:, o3:]], axis=1).astype(BF16)
            x = _mixer_ab(x, win, a_sinks[j].astype(F32), b_conv_w[j], _row(b_conv_b[j]),
                          _block_diag(b_wa[j]).astype(BF16), _row(b_ba[j]),
                          _block_diag(b_wx[j]).astype(BF16), _row(b_bx[j]), _row(b_lam[j]),
                          ab_w_out[j].astype(BF16), _row(ln_g[i, 1]), _row(ln_b[i, 1]))
        else:
            w = c_w_in[j]
            w1 = w[:, :4 * C_WIDTH].astype(BF16)
            w2 = jnp.pad(w[:, 4 * C_WIDTH:], ((0, 0), (0, LANES - 2 * C_HEADS))).astype(BF16)
            pad_gate = lambda v: jnp.pad(v.astype(F32), (C_GATE_COL, LANES - C_GATE_COL - C_HEADS)).reshape(1, LANES)
            x = _mixer_c(x, w1, w2, c_conv_w[j], pad_gate(c_a_log[j]), pad_gate(c_dt_bias[j]),
                         _row(c_norm_g[j]), c_w_out[j].astype(BF16), _row(ln_g[i, 1]), _row(ln_b[i, 1]))
        x = _ffn(x.reshape(n, d), i, *ffn2, _row(ln_g[i, 2]), _row(ln_b[i, 2]),
                 ple_args=(p_all, ple_w[0], _row(ple_bg[i]), ple_w[1])).reshape(b, s, d)
    return x
```

```python
import functools

import jax
import jax.numpy as jnp
from jax import lax
from jax.experimental import pallas as pl
from jax.experimental.pallas import tpu as pltpu

F32 = jnp.float32
BF16 = jnp.bfloat16

DEPTH = 2
CHUNK = 64
A_HEADS = 8
A_KV_HEADS = 2
A_HEAD_DIM = 64
A_WIDTH = A_HEADS * A_HEAD_DIM
A_KV_WIDTH = A_KV_HEADS * A_HEAD_DIM
A_PREV_CHUNKS = 2
A_BAND = (A_PREV_CHUNKS + 1) * CHUNK
B_WIDTH = 512
B_BLOCKS = 8
B_CONV = 4
RG_C = 8.0
C_HEADS = 8
C_HEAD_DIM = 128
C_WIDTH = C_HEADS * C_HEAD_DIM
C_CONV = 4
DN_ALPHA = (2.0 * DEPTH) ** 0.25
LN_EPS = 1e-5
NORM_EPS = 1e-6
NEG = -1e30

LANES = 128
SUBLANES = 8
FFN_ROWS = 1024
FFN_COLS = 256
FFN_OUT_ROWS = 256
MIX_BATCH = 8
C_GROUP = 8
A_GROUP_SEQ = 8
CONV_COLS = 512
VMEM_LIMIT = 56 * 1024 * 1024


def _dot(a, b):
    return jnp.dot(a, b, preferred_element_type=F32)


def _dot_nt(a, b):
    return lax.dot_general(a, b, (((1,), (1,)), ((), ())), preferred_element_type=F32)


def _dot_tn(a, b):
    return lax.dot_general(a, b, (((0,), (0,)), ((), ())), preferred_element_type=F32)


def _layer_norm(z, g, b):
    mu = jnp.mean(z, axis=-1, keepdims=True)
    d = z - mu
    var = jnp.mean(d * d, axis=-1, keepdims=True)
    return d * lax.rsqrt(var + LN_EPS) * g + b


def _silu(x):
    return x * jax.nn.sigmoid(x)


def _softplus(x):
    e = jnp.exp(-jnp.abs(x))
    u = 1.0 + e
    log1p_e = jnp.where(u == 1.0, e, jnp.log(u) * (e / (u - 1.0)))
    return jnp.maximum(x, 0.0) + log1p_e


def _neg_expm1(y):
    return -jnp.tanh(0.5 * y) * (jnp.exp(y) + 1.0)


def _causal_conv(cur, tail, w):
    taps = w.shape[0]
    b, t, c = cur.shape
    nt = t // SUBLANES
    tiles = jnp.concatenate([tail, cur], axis=1).reshape(b * (nt + 1), SUBLANES, c)
    sub = lax.broadcasted_iota(jnp.int32, (b, nt, SUBLANES, c), 2)
    out = w[taps - 1:taps].reshape(1, 1, -1) * cur
    for d in range(1, taps):
        rot = pltpu.roll(tiles, d, 1).reshape(b, nt + 1, SUBLANES, c)
        shifted = jnp.where(sub < d, rot[:, 0:nt], rot[:, 1:nt + 1]).reshape(b, t, c)
        out = out + w[taps - 1 - d:taps - d].reshape(1, 1, -1) * shifted
    return out


def _resident(shape):
    zeros = (0,) * len(shape)
    return pl.BlockSpec(shape, lambda *_: zeros, pipeline_mode=pl.Buffered(1))


def _tile_rows(ref, start, stop):
    if len(ref.shape) == 2:
        return ref[start:stop, :]
    return ref[start // CHUNK:stop // CHUNK].reshape(stop - start, ref.shape[-1])


def _ffn_steps(x_ref, wg_ref, wu_ref, wd_ref, lng_ref, lnb_ref, h_scr, out_ref):
    rows = h_scr.shape[0]
    cache = {}

    def xb():
        if "xb" not in cache:
            cache["xb"] = _tile_rows(x_ref, 0, rows).astype(BF16)
        return cache["xb"]

    def up(c):
        sl = slice(c * FFN_COLS, (c + 1) * FFN_COLS)
        g = _dot(xb(), wg_ref[:, sl])
        u = _dot(xb(), wu_ref[:, sl])
        h_scr[:, sl] = (_silu(g) * u).astype(BF16)

    def down(r):
        r0, r1 = r * FFN_OUT_ROWS, (r + 1) * FFN_OUT_ROWS
        y = _dot(h_scr[r0:r1, :], wd_ref[...])
        out_ref[r0:r1, :] = _layer_norm(DN_ALPHA * _tile_rows(x_ref, r0, r1) + 0.5 * y, lng_ref[...], lnb_ref[...])

    return ([functools.partial(up, c) for c in range(wg_ref.shape[-1] // FFN_COLS)]
            + [functools.partial(down, r) for r in range(rows // FFN_OUT_ROWS)])


def _ffn_rows(*refs):
    for step in _ffn_steps(*refs):
        step()


def _ffn_body(x_ref, wg_ref, wu_ref, wd_ref, lng_ref, lnb_ref, *rest, ple):
    if ple:
        p_ref, pwg_ref, pbg_ref, pwp_ref, o_ref, h_scr = rest
    else:
        o_ref, h_scr = rest
    _ffn_rows(x_ref, wg_ref, wu_ref, wd_ref, lng_ref, lnb_ref, h_scr, o_ref)
    if ple:
        out = o_ref[...]
        gate = jax.nn.sigmoid(_dot(out.astype(BF16), pwg_ref[...]) + pbg_ref[...])
        o_ref[...] = out + gate * _dot(p_ref[...].astype(BF16), pwp_ref[...])


def _ffn(x2, layer, wg, wu, wd, lng, lnb, ple_args=None):
    n, d = x2.shape
    f = wg.shape[-1]
    rows = min(FFN_ROWS, n)
    assert n % rows == 0 and f % FFN_COLS == 0 and rows % FFN_OUT_ROWS == 0
    row_spec = pl.BlockSpec((rows, d), lambda i: (i, 0))

    def of_layer(shape):
        return pl.BlockSpec((None,) + shape, lambda i: (layer,) + (0,) * len(shape), pipeline_mode=pl.Buffered(1))

    in_specs = [row_spec, of_layer((d, f)), of_layer((d, f)), of_layer((f, d)),
                _resident((1, d)), _resident((1, d))]
    args = [x2, wg, wu, wd, lng, lnb]
    if ple_args is not None:
        p_all, pwg, pbg, pwp = ple_args
        dp = p_all.shape[-1]
        in_specs += [pl.BlockSpec((None, rows, dp), lambda i: (layer, i, 0)), of_layer((d, d)),
                     _resident((1, d)), of_layer((dp, d))]
        args += [p_all, pwg, pbg, pwp]
    return pl.pallas_call(
        functools.partial(_ffn_body, ple=ple_args is not None),
        out_shape=jax.ShapeDtypeStruct((n, d), F32),
        grid=(n // rows,),
        in_specs=in_specs,
        out_specs=row_spec,
        scratch_shapes=[pltpu.VMEM((rows, f), BF16)],
        compiler_params=pltpu.CompilerParams(dimension_semantics=("arbitrary",),
                                             vmem_limit_bytes=VMEM_LIMIT),
        name="ffn_ple" if ple_args is not None else "ffn",
    )(*args)


AB_Q0 = 0
AB_K0 = A_WIDTH
AB_V0 = AB_K0 + 2 * A_KV_WIDTH
AB_X0 = AB_V0 + 2 * A_KV_WIDTH
AB_G0 = AB_X0 + B_WIDTH
AB_COLS = AB_G0 + B_WIDTH


def _mixer_ab_tile(n, reset, x, sinks_ref, win_ref, convw_ref, convb_ref, wa_ref, ba_ref, wx_ref, bx_ref,
                   lam_ref, wout_ref, lng_ref, lnb_ref, proj_scr, kbuf, vbuf, ctail, hcarry, y_scr, *, bb, group,
                   side=()):
    rows = bb * CHUNK
    single_pass = bb == group
    side = list(side)

    def fill(count):
        for _ in range(min(count, len(side))):
            side.pop(0)()

    @pl.when(reset)
    def _():
        kbuf[...] = jnp.zeros_like(kbuf)
        vbuf[...] = jnp.zeros_like(vbuf)
        ctail[...] = jnp.zeros_like(ctail)
        hcarry[...] = jnp.zeros_like(hcarry)

    proj_scr[...] = _dot(x.astype(BF16), win_ref[...])
    fill(2)

    lo = lax.broadcasted_iota(jnp.int32, (rows, LANES), 1) < A_HEAD_DIM

    def masked_variants(c0):
        ta = proj_scr[:, c0:c0 + LANES]
        tb = proj_scr[:, c0 + LANES:c0 + 2 * LANES]
        zero = jnp.zeros_like(ta)
        parts = [jnp.where(lo, ta, zero), jnp.where(lo, zero, tb),
                 jnp.where(lo, tb, zero), jnp.where(lo, zero, ta)]
        return jnp.concatenate(parts, axis=1).astype(BF16).reshape(bb, CHUNK, 4 * LANES)

    old_k = kbuf[:, CHUNK:A_BAND, :]
    kbuf[:, 0:A_BAND - CHUNK, :] = old_k
    kbuf[:, A_BAND - CHUNK:A_BAND, :] = masked_variants(AB_K0)
    old_v = vbuf[:, CHUNK:A_BAND, :]
    vbuf[:, 0:A_BAND - CHUNK, :] = old_v
    vbuf[:, A_BAND - CHUNK:A_BAND, :] = masked_variants(AB_V0)

    qi = lax.broadcasted_iota(jnp.int32, (2 * CHUNK, A_BAND), 0)
    ki = lax.broadcasted_iota(jnp.int32, (2 * CHUNK, A_BAND), 1)
    first_tile = qi < CHUNK
    dist = jnp.abs(jnp.where(first_tile, qi, qi - CHUNK) + (A_BAND - CHUNK) - ki).astype(F32)
    valid = ki >= (A_BAND - CHUNK) - CHUNK * n
    row_first = lax.broadcasted_iota(jnp.int32, (2 * CHUNK, 1), 0) < CHUNK
    scale = A_HEAD_DIM ** -0.5

    def head_consts(kv, parity):
        ha = 4 * kv + parity
        hb = ha + 2
        slope = jnp.where(row_first, 2.0 ** (-8.0 * (ha + 1) / A_HEADS), 2.0 ** (-8.0 * (hb + 1) / A_HEADS))
        sink = jnp.where(row_first, sinks_ref[ha], sinks_ref[hb])
        return slope * dist, sink

    consts = [[head_consts(kv, par) for par in range(2)] for kv in range(A_KV_HEADS)]

    def attend(it, carry):
        combos = [(it * group + sq, kv, par) for sq in range(group) for kv in range(A_KV_HEADS) for par in range(2)]
        idx = range(len(combos))
        rsl = [pl.ds(pl.multiple_of(b * CHUNK, CHUNK), CHUNK) for b, _, _ in combos]
        tile = [slice((2 * kv + par) * LANES, (2 * kv + par + 1) * LANES) for _, kv, par in combos]
        bias = [consts[kv][par][0] for _, kv, par in combos]
        sink = [consts[kv][par][1] for _, kv, par in combos]
        q2 = {}
        for i, (b, kv, par) in enumerate(combos):
            if par == 0:
                c0 = AB_Q0 + 2 * LANES * kv
                qq = jnp.concatenate([proj_scr[rsl[i], c0:c0 + LANES], proj_scr[rsl[i], c0 + LANES:c0 + 2 * LANES]],
                                     axis=0)
                q2[i] = q2[i + 1] = (qq * scale).astype(BF16)
        s = [_dot_nt(q2[i], kbuf[b, :, tile[i]]) for i, (b, _, _) in enumerate(combos)]
        if single_pass:
            fill(3)
        s = [jnp.where(valid, s[i] - bias[i], NEG) for i in idx]
        m = [jnp.maximum(jnp.max(s[i], axis=-1, keepdims=True), sink[i]) for i in idx]
        pr = [jnp.exp(s[i] - m[i]) for i in idx]
        den = [jnp.sum(pr[i], axis=-1, keepdims=True) + jnp.exp(sink[i] - m[i]) for i in idx]
        pv = [_dot(pr[i].astype(BF16), vbuf[b, :, tile[i]]) for i, (b, _, _) in enumerate(combos)]
        if single_pass:
            fill(1)
        for i, (b, kv, par) in enumerate(combos):
            if par == 0:
                c0 = AB_Q0 + 2 * LANES * kv
                acc = pv[i] / den[i] + pv[i + 1] / den[i + 1]
                y_scr[rsl[i], c0:c0 + LANES] = acc[0:CHUNK]
                y_scr[rsl[i], c0 + LANES:c0 + 2 * LANES] = acc[CHUNK:2 * CHUNK]
        return carry

    if single_pass:
        attend(0, 0)
    else:
        lax.fori_loop(0, bb // group, attend, 0)

    bx3 = proj_scr[:, AB_X0:AB_X0 + B_WIDTH].reshape(bb, CHUNK, B_WIDTH)
    conv = _causal_conv(bx3, ctail[...], convw_ref[...]) + convb_ref[...].reshape(1, 1, B_WIDTH)
    ctail[...] = bx3[:, CHUNK - SUBLANES:CHUNK, :]
    c2 = conv.reshape(rows, B_WIDTH)
    cb = c2.astype(BF16)
    r = jax.nn.sigmoid(_dot(cb, wa_ref[...]) + ba_ref[...])
    i = jax.nn.sigmoid(_dot(cb, wx_ref[...]) + bx_ref[...])
    fill(5)
    log_a = (-RG_C) * r * _softplus(-lam_ref[...])
    n_tiles = CHUNK // SUBLANES
    a = jnp.exp(log_a).reshape(bb * n_tiles, SUBLANES, B_WIDTH)
    u = (jnp.sqrt(_neg_expm1(2.0 * log_a)) * (i * c2)).reshape(bb * n_tiles, SUBLANES, B_WIDTH)
    ti = lax.broadcasted_iota(jnp.int32, (bb * n_tiles, SUBLANES, B_WIDTH), 1)
    step = 1
    while step < SUBLANES:
        keep = ti >= step
        a_prev = jnp.where(keep, pltpu.roll(a, step, 1), 1.0)
        u_prev = jnp.where(keep, pltpu.roll(u, step, 1), 0.0)
        u = u + a * u_prev
        a = a * a_prev
        step *= 2
    a = a.reshape(bb, n_tiles, SUBLANES, B_WIDTH)
    u = u.reshape(bb, n_tiles, SUBLANES, B_WIDTH)
    carry = hcarry[...].reshape(bb, 1, B_WIDTH)
    h_tiles = []
    for t in range(n_tiles):
        h_t = u[:, t] + a[:, t] * carry
        carry = h_t[:, SUBLANES - 1:SUBLANES, :]
        h_tiles.append(h_t)
    hcarry[...] = carry.reshape(bb, B_WIDTH)
    h = jnp.concatenate(h_tiles, axis=1)
    gate = jax.nn.gelu(proj_scr[:, AB_G0:AB_G0 + B_WIDTH], approximate=True)
    y_scr[:, A_WIDTH:A_WIDTH + B_WIDTH] = h.reshape(rows, B_WIDTH) * gate

    y = _dot(y_scr[...].astype(BF16), wout_ref[...])
    fill(len(side))
    return _layer_norm(DN_ALPHA * x + y, lng_ref[...], lnb_ref[...])


def _mixer_ab_body(sinks_ref, x_ref, *rest, bb, group):
    weights, o_ref, scratch = rest[:11], rest[11], rest[12:]
    n = pl.program_id(1)
    x = x_ref[...].reshape(bb * CHUNK, x_ref.shape[-1])
    out = _mixer_ab_tile(n, n == 0, x, sinks_ref, *weights, *scratch, bb=bb, group=group)
    o_ref[...] = out.reshape(o_ref.shape)


def _ffn_mixer_ab_body(sinks_ref, x_ref, wg_ref, wu_ref, wd_ref, lng1_ref, lnb1_ref, *rest, bb, group):
    weights, o_ref, h_scr, f1_scr, scratch = rest[:11], rest[11], rest[12], rest[13], rest[14:]
    j = pl.program_id(1)

    @pl.when(j == 0)
    def _():
        f1_scr[...] = jnp.zeros_like(f1_scr)

    prev = f1_scr[(j + 1) % 2]
    ffn = _ffn_steps(x_ref, wg_ref, wu_ref, wd_ref, lng1_ref, lnb1_ref, h_scr, f1_scr.at[j % 2])
    out = _mixer_ab_tile(j - 1, j <= 1, prev, sinks_ref, *weights, *scratch, bb=bb, group=group, side=ffn)
    o_ref[...] = out.reshape(o_ref.shape)


def _ab_scratch(bb):
    rows = bb * CHUNK
    return [
        pltpu.VMEM((rows, AB_COLS), F32),
        pltpu.VMEM((bb, A_BAND, 4 * LANES), BF16),
        pltpu.VMEM((bb, A_BAND, 4 * LANES), BF16),
        pltpu.VMEM((bb, SUBLANES, B_WIDTH), F32),
        pltpu.VMEM((bb, B_WIDTH), F32),
        pltpu.VMEM((rows, A_WIDTH + B_WIDTH), F32),
    ]


def _ffn_mixer_ab(x3, layer, wg, wu, wd, lng1, lnb1, win, sinks, convw, convb, wa, ba, wx, bx, lam, wout, lng, lnb):
    b, s, d = x3.shape
    f = wg.shape[-1]
    bb = min(MIX_BATCH, b)
    group = min(A_GROUP_SEQ, bb)
    n_chunks = s // CHUNK
    rows = bb * CHUNK
    assert b % bb == 0 and s % CHUNK == 0 and bb % group == 0 and rows % FFN_OUT_ROWS == 0 and f % FFN_COLS == 0
    in_tile = pl.BlockSpec((bb, CHUNK, d), lambda i, j: (i, jnp.minimum(j, n_chunks - 1), 0))
    out_tile = pl.BlockSpec((bb, CHUNK, d), lambda i, j: (i, jnp.maximum(j - 1, 0), 0))

    def of_layer(shape):
        return pl.BlockSpec((None,) + shape, lambda i, j: (layer,) + (0,) * len(shape), pipeline_mode=pl.Buffered(1))

    mixer_w = (win, convw, convb, wa, ba, wx, bx, lam, wout, lng, lnb)
    in_specs = ([pl.BlockSpec(memory_space=pltpu.SMEM), in_tile, of_layer((d, f)), of_layer((d, f)),
                 of_layer((f, d)), _resident(lng1.shape), _resident(lnb1.shape)]
                + [_resident(w.shape) for w in mixer_w])
    return pl.pallas_call(
        functools.partial(_ffn_mixer_ab_body, bb=bb, group=group),
        out_shape=jax.ShapeDtypeStruct((b, s, d), F32),
        grid=(b // bb, n_chunks + 1),
        in_specs=in_specs,
        out_specs=out_tile,
        scratch_shapes=[pltpu.VMEM((rows, f), BF16), pltpu.VMEM((2, rows, d), F32)] + _ab_scratch(bb),
        compiler_params=pltpu.CompilerParams(dimension_semantics=("arbitrary", "arbitrary"),
                                             vmem_limit_bytes=VMEM_LIMIT),
        name="ffn_mixer_ab",
    )(sinks, x3, wg, wu, wd, lng1, lnb1, *mixer_w)


def _mixer_ab(x3, win, sinks, convw, convb, wa, ba, wx, bx, lam, wout, lng, lnb):
    b, s, d = x3.shape
    bb = min(MIX_BATCH, b)
    group = min(A_GROUP_SEQ, bb)
    assert b % bb == 0 and s % CHUNK == 0 and bb % group == 0
    rows = bb * CHUNK
    tile = pl.BlockSpec((bb, CHUNK, d), lambda i, j: (i, j, 0))
    in_specs = [pl.BlockSpec(memory_space=pltpu.SMEM), tile, _resident(win.shape),
                _resident(convw.shape), _resident(convb.shape), _resident(wa.shape), _resident(ba.shape),
                _resident(wx.shape), _resident(bx.shape), _resident(lam.shape), _resident(wout.shape),
                _resident(lng.shape), _resident(lnb.shape)]
    return pl.pallas_call(
        functools.partial(_mixer_ab_body, bb=bb, group=group),
        out_shape=jax.ShapeDtypeStruct((b, s, d), F32),
        grid=(b // bb, s // CHUNK),
        in_specs=in_specs,
        out_specs=tile,
        scratch_shapes=_ab_scratch(bb),
        compiler_params=pltpu.CompilerParams(dimension_semantics=("arbitrary", "arbitrary"),
                                             vmem_limit_bytes=VMEM_LIMIT),
        name="mixer_ab",
    )(sinks, x3, win, convw, convb, wa, ba, wx, bx, lam, wout, lng, lnb)


C_GATE_COL = C_HEADS


def _mixer_c_body(x_ref, w1_ref, w2_ref, convw_ref, alog_ref, dtb_ref, ng_ref, wout_ref, lng_ref, lnb_ref,
                  o_ref, ctail, qkv_scr, z_scr, ba_scr, gcol_scr, grow_scr, state, og_scr, *, bb, group):
    n = pl.program_id(1)
    rows = bb * CHUNK
    d_model = x_ref.shape[-1]
    qkv_w = 3 * C_WIDTH

    @pl.when(n == 0)
    def _():
        ctail[...] = jnp.zeros_like(ctail)
        state[...] = jnp.zeros_like(state)

    x = x_ref[...].reshape(rows, d_model)
    xb = x.astype(BF16)
    for c in range(qkv_w // CONV_COLS):
        csl = slice(c * CONV_COLS, (c + 1) * CONV_COLS)
        cur = _dot(xb, w1_ref[:, csl]).reshape(bb, CHUNK, CONV_COLS)
        conv = _causal_conv(cur, ctail[:, :, csl], convw_ref[:, csl])
        ctail[:, :, csl] = cur[:, CHUNK - SUBLANES:CHUNK, :]
        qkv_scr[:, csl] = _silu(conv).reshape(rows, CONV_COLS)
    z_scr[...] = _dot(xb, w1_ref[:, qkv_w:qkv_w + C_WIDTH])
    ba_scr[...] = _dot(xb, w2_ref[...])

    ri = lax.broadcasted_iota(jnp.int32, (CHUNK, CHUNK), 0)
    ci = lax.broadcasted_iota(jnp.int32, (CHUNK, CHUNK), 1)
    tril = ri >= ci
    ri2 = lax.broadcasted_iota(jnp.int32, (CHUNK, LANES), 0)
    li2 = lax.broadcasted_iota(jnp.int32, (CHUNK, LANES), 1)
    ci2 = jnp.bitwise_and(li2, CHUNK - 1)
    lane_lo = li2 < CHUNK
    tril2 = ri2 >= ci2
    strict2 = ri2 > ci2
    eye2 = (ri2 == ci2).astype(F32)
    keep_lo = jnp.where(lane_lo, 1.0, 0.0).astype(BF16)
    keep_hi = jnp.where(lane_lo, 0.0, 1.0).astype(BF16)
    zero_t = jnp.zeros((CHUNK, C_HEAD_DIM), BF16)
    neg_rate = -jnp.exp(alog_ref[...])
    dtb = dtb_ref[...]
    norm_g = ng_ref[...]
    qscale = C_HEAD_DIM ** -0.5

    bab = ba_scr[...]
    g_all = neg_rate * _softplus(bab + dtb)
    ba_scr[...] = jax.nn.sigmoid(bab)
    g_hi = g_all.astype(BF16)
    rem = g_all - g_hi.astype(F32)
    g_mid = rem.astype(BF16)
    g_lo = (rem - g_mid.astype(F32)).astype(BF16)
    pieces = jnp.concatenate([part[b * CHUNK:(b + 1) * CHUNK] for part in (g_hi, g_mid, g_lo) for b in range(bb)],
                             axis=1)
    csum = _dot(tril.astype(BF16), pieces)
    for b in range(bb):
        gcb = (csum[:, b * LANES:(b + 1) * LANES] + csum[:, (bb + b) * LANES:(bb + b + 1) * LANES]
               + csum[:, (2 * bb + b) * LANES:(2 * bb + b + 1) * LANES])
        gcol_scr[b * CHUNK:(b + 1) * CHUNK, :] = gcb
        grow_scr[b * LANES:(b + 1) * LANES, :] = jnp.concatenate([gcb, gcb], axis=0).T

    def unit_norm(t, extra):
        return t * (lax.rsqrt(jnp.sum(t * t, axis=-1, keepdims=True) + NORM_EPS) * extra)

    def head_cols(part, h):
        return slice(part * C_WIDTH + h * C_HEAD_DIM, part * C_WIDTH + (h + 1) * C_HEAD_DIM)

    def lane_cat(a, b):
        return jnp.concatenate([a, b], axis=1)

    def block_diag2(top, bottom):
        return jnp.concatenate([lane_cat(top, jnp.zeros_like(top)), lane_cat(jnp.zeros_like(bottom), bottom)], axis=0)

    def unit(it, carry):
        units = [(it * group + s, p) for s in range(group) for p in range(C_HEADS // 2)]
        idx = range(len(units))
        rsl = [pl.ds(pl.multiple_of(b * CHUNK, CHUNK), CHUNK) for b, _ in units]
        heads = [(2 * p, 2 * p + 1) for _, p in units]
        st = [[state[b * C_HEADS + h] for h in heads[i]] for i, (b, _) in enumerate(units)]
        beta = [[ba_scr[rsl[i], h:h + 1] for h in heads[i]] for i in idx]
        gc = [[gcol_scr[rsl[i], C_GATE_COL + h:C_GATE_COL + h + 1] for h in heads[i]] for i in idx]
        gr = [[grow_scr[pl.ds(pl.multiple_of(b * LANES, LANES) + C_GATE_COL + h, 1), :] for h in heads[i]]
              for i, (b, _) in enumerate(units)]
        gl = [[gc[i][j][CHUNK - 1:CHUNK, :] for j in range(2)] for i in idx]
        eg = [[jnp.exp(gc[i][j]) for j in range(2)] for i in idx]
        q = [[unit_norm(qkv_scr[rsl[i], head_cols(0, h)], qscale) for h in heads[i]] for i in idx]
        k = [[unit_norm(qkv_scr[rsl[i], head_cols(1, h)], 1.0) for h in heads[i]] for i in idx]
        v = [[qkv_scr[rsl[i], head_cols(2, h)] for h in heads[i]] for i in idx]
        kb = [[k[i][j] * beta[i][j] for j in range(2)] for i in idx]
        lhs = [jnp.concatenate([lane_cat(q[i][0], q[i][1]), lane_cat(kb[i][0], kb[i][1])], axis=0).astype(BF16)
               for i in idx]
        kbd = [block_diag2(k[i][0].astype(BF16), k[i][1].astype(BF16)) for i in idx]
        qk = [_dot_nt(lhs[i], kbd[i]) for i in idx]
        gc2 = [jnp.where(lane_lo, gc[i][0], gc[i][1]) for i in idx]
        gr2 = [jnp.where(lane_lo[0:1], gr[i][0], gr[i][1]) for i in idx]
        decay = [jnp.where(tril2, jnp.exp(jnp.where(tril2, gc2[i] - gr2[i], 0.0)), 0.0) for i in idx]
        attn = [(qk[i][0:CHUNK] * decay[i]).astype(BF16) for i in idx]
        lmat = [jnp.where(strict2, qk[i][CHUNK:2 * CHUNK] * decay[i], 0.0) for i in idx]
        tinv = [eye2 - lmat[i] for i in idx]
        pw = [lmat[i].astype(BF16) for i in idx]
        for _ in range(5):
            pw = [_dot(pw[i], jnp.concatenate([pw[i] * keep_lo, pw[i] * keep_hi], axis=0)).astype(BF16) for i in idx]
            tinv = [tinv[i] + _dot(tinv[i].astype(BF16), jnp.concatenate([pw[i] * keep_lo, pw[i] * keep_hi], axis=0))
                    for i in idx]
        rhs = [jnp.concatenate([
            jnp.concatenate([(v[i][0] * beta[i][0]).astype(BF16), zero_t, (kb[i][0] * eg[i][0]).astype(BF16), zero_t], axis=1),
            jnp.concatenate([zero_t, (v[i][1] * beta[i][1]).astype(BF16), zero_t, (kb[i][1] * eg[i][1]).astype(BF16)], axis=1),
        ], axis=0) for i in idx]
        uw = [_dot(tinv[i].astype(BF16), rhs[i]) for i in idx]
        sbd = [block_diag2(st[i][0].astype(BF16), st[i][1].astype(BF16)) for i in idx]
        wq = [_dot(jnp.concatenate([uw[i][:, 2 * C_HEAD_DIM:],
                                    lane_cat(q[i][0] * eg[i][0], q[i][1] * eg[i][1])], axis=0).astype(BF16), sbd[i])
              for i in idx]
        vnb = [(uw[i][:, 0:2 * C_HEAD_DIM] - wq[i][0:CHUNK]).astype(BF16) for i in idx]
        vbd = [block_diag2(vnb[i][:, 0:C_HEAD_DIM], vnb[i][:, C_HEAD_DIM:]) for i in idx]
        o = [wq[i][CHUNK:2 * CHUNK] + _dot(attn[i], vbd[i]) for i in idx]
        for i, (b, _) in enumerate(units):
            for j, h in enumerate(heads[i]):
                hs = slice(j * C_HEAD_DIM, (j + 1) * C_HEAD_DIM)
                kdec = (k[i][j] * jnp.exp(gl[i][j] - gc[i][j])).astype(BF16)
                state[b * C_HEADS + h] = st[i][j] * jnp.exp(gl[i][j]) + _dot_tn(kdec, vnb[i][:, hs])
                oh = o[i][:, hs]
                on = oh * lax.rsqrt(jnp.mean(oh * oh, axis=-1, keepdims=True) + NORM_EPS) * norm_g
                og_scr[rsl[i], head_cols(0, h)] = on * _silu(z_scr[rsl[i], head_cols(0, h)])
        return carry

    lax.fori_loop(0, bb // group, unit, 0)

    y = _dot(og_scr[...].astype(BF16), wout_ref[...])
    out = _layer_norm(DN_ALPHA * x + y, lng_ref[...], lnb_ref[...])
    o_ref[...] = out.reshape(bb, CHUNK, d_model)


def _mixer_c(x3, w1, w2, convw, alog, dtb, ng, wout, lng, lnb):
    b, s, d = x3.shape
    bb = min(MIX_BATCH, b)
    group = min(C_GROUP, bb)
    assert b % bb == 0 and s % CHUNK == 0 and bb % group == 0
    rows = bb * CHUNK
    tile = pl.BlockSpec((bb, CHUNK, d), lambda i, j: (i, j, 0))
    in_specs = [tile] + [_resident(a.shape) for a in (w1, w2, convw, alog, dtb, ng, wout, lng, lnb)]
    return pl.pallas_call(
        functools.partial(_mixer_c_body, bb=bb, group=group),
        out_shape=jax.ShapeDtypeStruct((b, s, d), F32),
        grid=(b // bb, s // CHUNK),
        in_specs=in_specs,
        out_specs=tile,
        scratch_shapes=[
            pltpu.VMEM((bb, SUBLANES, 3 * C_WIDTH), F32),
            pltpu.VMEM((rows, 3 * C_WIDTH), F32),
            pltpu.VMEM((rows, C_WIDTH), F32),
            pltpu.VMEM((rows, LANES), F32),
            pltpu.VMEM((rows, LANES), F32),
            pltpu.VMEM((bb * LANES, LANES), F32),
            pltpu.VMEM((bb * C_HEADS, C_HEAD_DIM, C_HEAD_DIM), F32),
            pltpu.VMEM((rows, C_WIDTH), F32),
        ],
        compiler_params=pltpu.CompilerParams(dimension_semantics=("arbitrary", "arbitrary"),
                                             vmem_limit_bytes=VMEM_LIMIT),
        name="mixer_c",
    )(x3, w1, w2, convw, alog, dtb, ng, wout, lng, lnb)


def _block_diag(w):
    nb, bs, _ = w.shape
    out = jnp.zeros((nb * bs, nb * bs), w.dtype)
    for i in range(nb):
        out = out.at[i * bs:(i + 1) * bs, i * bs:(i + 1) * bs].set(w[i])
    return out


def _row(v):
    return v.reshape(1, -1).astype(F32)


def kernel(x, p, ffn1_wg, ffn1_wu, ffn1_wd, ffn2_wg, ffn2_wu, ffn2_wd, ln_g, ln_b, ple_wg, ple_bg, ple_wp, ab_w_in, a_sinks, b_conv_w, b_conv_b, b_wa, b_ba, b_wx, b_bx, b_lam, ab_w_out, c_w_in, c_conv_w, c_a_log, c_dt_bias, c_norm_g, c_w_out):
    b, s, d = x.shape
    n = b * s
    ffn1 = [w.astype(BF16) for w in (ffn1_wg, ffn1_wu, ffn1_wd)]
    ffn2 = [w.astype(BF16) for w in (ffn2_wg, ffn2_wu, ffn2_wd)]
    ple_w = (ple_wg.astype(BF16), ple_wp.astype(BF16))
    p_all = p.reshape(DEPTH, n, -1)
    for i in range(DEPTH):
        j = i // 2
        if i % 2 == 0:
            w = ab_w_in[j]
            o1, o2, o3 = A_WIDTH, A_WIDTH + A_KV_WIDTH, A_WIDTH + 2 * A_KV_WIDTH
            hd = A_HEAD_DIM
            swap = lambda m: jnp.concatenate([m[:, hd:2 * hd], m[:, 0:hd]], axis=1)
            win = jnp.concatenate([w[:, :o1], w[:, o1:o2], swap(w[:, o1:o2]), w[:, o2:o3], swap(w[:, o2:o3]),
                                   w[:, o3:]], axis=1).astype(BF16)
            x = _ffn_mixer_ab(x, i, *ffn1, _row(ln_g[i, 0]), _row(ln_b[i, 0]),
                              win, a_sinks[j].astype(F32), b_conv_w[j], _row(b_conv_b[j]),
                              _block_diag(b_wa[j]).astype(BF16), _row(b_ba[j]),
                              _block_diag(b_wx[j]).astype(BF16), _row(b_bx[j]), _row(b_lam[j]),
                              ab_w_out[j].astype(BF16), _row(ln_g[i, 1]), _row(ln_b[i, 1]))
        else:
            x = _ffn(x.reshape(n, d), i, *ffn1, _row(ln_g[i, 0]), _row(ln_b[i, 0])).reshape(b, s, d)
            w = c_w_in[j]
            w1 = w[:, :4 * C_WIDTH].astype(BF16)
            w2 = jnp.pad(w[:, 4 * C_WIDTH:], ((0, 0), (0, LANES - 2 * C_HEADS))).astype(BF16)
            pad_gate = lambda v: jnp.pad(v.astype(F32), (C_GATE_COL, LANES - C_GATE_COL - C_HEADS)).reshape(1, LANES)
            x = _mixer_c(x, w1, w2, c_conv_w[j], pad_gate(c_a_log[j]), pad_gate(c_dt_bias[j]),
                         _row(c_norm_g[j]), c_w_out[j].astype(BF16), _row(ln_g[i, 1]), _row(ln_b[i, 1]))
        x = _ffn(x.reshape(n, d), i, *ffn2, _row(ln_g[i, 2]), _row(ln_b[i, 2]),
                 ple_args=(p_all, ple_w[0], _row(ple_bg[i]), ple_w[1])).reshape(b, s, d)
    return x
```

```python
import functools

import jax
import jax.numpy as jnp
from jax import lax
from jax.experimental import pallas as pl
from jax.experimental.pallas import tpu as pltpu

F32 = jnp.float32
BF16 = jnp.bfloat16

DEPTH = 2
CHUNK = 64
A_HEADS = 8
A_KV_HEADS = 2
A_HEAD_DIM = 64
A_WIDTH = A_HEADS * A_HEAD_DIM
A_KV_WIDTH = A_KV_HEADS * A_HEAD_DIM
A_PREV_CHUNKS = 2
A_BAND = (A_PREV_CHUNKS + 1) * CHUNK
B_WIDTH = 512
B_BLOCKS = 8
B_CONV = 4
RG_C = 8.0
C_HEADS = 8
C_HEAD_DIM = 128
C_WIDTH = C_HEADS * C_HEAD_DIM
C_CONV = 4
DN_ALPHA = (2.0 * DEPTH) ** 0.25
LN_EPS = 1e-5
NORM_EPS = 1e-6
NEG = -1e30

LANES = 128
SUBLANES = 8
FFN_ROWS = 1024
FFN_COLS = 256
FFN_OUT_ROWS = 256
MIX_BATCH = 8
A_GROUP_SEQ = 8
VMEM_LIMIT = 56 * 1024 * 1024


def _dot(a, b):
    return jnp.dot(a, b, preferred_element_type=F32)


def _dot_nt(a, b):
    return lax.dot_general(a, b, (((1,), (1,)), ((), ())), preferred_element_type=F32)


def _dot_tn(a, b):
    return lax.dot_general(a, b, (((0,), (0,)), ((), ())), preferred_element_type=F32)


def _layer_norm(z, g, b):
    mu = jnp.mean(z, axis=-1, keepdims=True)
    d = z - mu
    var = jnp.mean(d * d, axis=-1, keepdims=True)
    return d * lax.rsqrt(var + LN_EPS) * g + b


def _silu(x):
    return x * jax.nn.sigmoid(x)


def _softplus(x):
    e = jnp.exp(-jnp.abs(x))
    u = 1.0 + e
    log1p_e = jnp.where(u == 1.0, e, jnp.log(u) * (e / (u - 1.0)))
    return jnp.maximum(x, 0.0) + log1p_e


def _neg_expm1(y):
    return -jnp.tanh(0.5 * y) * (jnp.exp(y) + 1.0)


def _causal_conv(cur, tail, w):
    taps = w.shape[0]
    b, t, c = cur.shape
    nt = t // SUBLANES
    tiles = jnp.concatenate([tail, cur], axis=1).reshape(b * (nt + 1), SUBLANES, c)
    sub = lax.broadcasted_iota(jnp.int32, (b, nt, SUBLANES, c), 2)
    out = w[taps - 1:taps].reshape(1, 1, -1) * cur
    for d in range(1, taps):
        rot = pltpu.roll(tiles, d, 1).reshape(b, nt + 1, SUBLANES, c)
        shifted = jnp.where(sub < d, rot[:, 0:nt], rot[:, 1:nt + 1]).reshape(b, t, c)
        out = out + w[taps - 1 - d:taps - d].reshape(1, 1, -1) * shifted
    return out


def _resident(shape):
    zeros = (0,) * len(shape)
    return pl.BlockSpec(shape, lambda *_: zeros, pipeline_mode=pl.Buffered(1))


def _tile_rows(ref, start, stop):
    if len(ref.shape) == 2:
        return ref[start:stop, :]
    return ref[start // CHUNK:stop // CHUNK].reshape(stop - start, ref.shape[-1])


def _ffn_steps(x_ref, wg_ref, wu_ref, wd_ref, lng_ref, lnb_ref, h_scr, out_ref):
    rows = h_scr.shape[0]
    cache = {}

    def xb():
        if "xb" not in cache:
            cache["xb"] = _tile_rows(x_ref, 0, rows).astype(BF16)
        return cache["xb"]

    def up(c):
        sl = slice(c * FFN_COLS, (c + 1) * FFN_COLS)
        g = _dot(xb(), wg_ref[:, sl])
        u = _dot(xb(), wu_ref[:, sl])
        h_scr[:, sl] = (_silu(g) * u).astype(BF16)

    def down(r):
        r0, r1 = r * FFN_OUT_ROWS, (r + 1) * FFN_OUT_ROWS
        y = _dot(h_scr[r0:r1, :], wd_ref[...])
        out_ref[r0:r1, :] = _layer_norm(DN_ALPHA * _tile_rows(x_ref, r0, r1) + 0.5 * y, lng_ref[...], lnb_ref[...])

    return ([functools.partial(up, c) for c in range(wg_ref.shape[-1] // FFN_COLS)]
            + [functools.partial(down, r) for r in range(rows // FFN_OUT_ROWS)])


def _ffn_rows(*refs):
    for step in _ffn_steps(*refs):
        step()


def _ffn_body(x_ref, wg_ref, wu_ref, wd_ref, lng_ref, lnb_ref, *rest, ple):
    if ple:
        p_ref, pwg_ref, pbg_ref, pwp_ref, o_ref, h_scr = rest
    else:
        o_ref, h_scr = rest
    _ffn_rows(x_ref, wg_ref, wu_ref, wd_ref, lng_ref, lnb_ref, h_scr, o_ref)
    if ple:
        out = o_ref[...]
        gate = jax.nn.sigmoid(_dot(out.astype(BF16), pwg_ref[...]) + pbg_ref[...])
        o_ref[...] = out + gate * _dot(p_ref[...].astype(BF16), pwp_ref[...])


def _ffn(x2, layer, wg, wu, wd, lng, lnb, ple_args=None):
    n, d = x2.shape
    f = wg.shape[-1]
    rows = min(FFN_ROWS, n)
    assert n % rows == 0 and f % FFN_COLS == 0 and rows % FFN_OUT_ROWS == 0
    row_spec = pl.BlockSpec((rows, d), lambda i: (i, 0))

    def of_layer(shape):
        return pl.BlockSpec((None,) + shape, lambda i: (layer,) + (0,) * len(shape), pipeline_mode=pl.Buffered(1))

    in_specs = [row_spec, of_layer((d, f)), of_layer((d, f)), of_layer((f, d)),
                _resident((1, d)), _resident((1, d))]
    args = [x2, wg, wu, wd, lng, lnb]
    if ple_args is not None:
        p_all, pwg, pbg, pwp = ple_args
        dp = p_all.shape[-1]
        in_specs += [pl.BlockSpec((None, rows, dp), lambda i: (layer, i, 0)), of_layer((d, d)),
                     _resident((1, d)), of_layer((dp, d))]
        args += [p_all, pwg, pbg, pwp]
    return pl.pallas_call(
        functools.partial(_ffn_body, ple=ple_args is not None),
        out_shape=jax.ShapeDtypeStruct((n, d), F32),
        grid=(n // rows,),
        in_specs=in_specs,
        out_specs=row_spec,
        scratch_shapes=[pltpu.VMEM((rows, f), BF16)],
        compiler_params=pltpu.CompilerParams(dimension_semantics=("arbitrary",),
                                             vmem_limit_bytes=VMEM_LIMIT),
        name="ffn_ple" if ple_args is not None else "ffn",
    )(*args)


AB_Q0 = 0
AB_K0 = A_WIDTH
AB_V0 = AB_K0 + 2 * A_KV_WIDTH
AB_X0 = AB_V0 + 2 * A_KV_WIDTH
AB_G0 = AB_X0 + B_WIDTH
AB_COLS = AB_G0 + B_WIDTH


def _mixer_ab_tile(n, reset, x, sinks_ref, win_ref, convw_ref, convb_ref, wa_ref, ba_ref, wx_ref, bx_ref,
                   lam_ref, wout_ref, lng_ref, lnb_ref, proj_scr, kbuf, vbuf, ctail, hcarry, y_scr, *, bb, group,
                   side=()):
    rows = bb * CHUNK
    single_pass = bb == group
    side = list(side)

    def fill(count):
        for _ in range(min(count, len(side))):
            side.pop(0)()

    @pl.when(reset)
    def _():
        kbuf[...] = jnp.zeros_like(kbuf)
        vbuf[...] = jnp.zeros_like(vbuf)
        ctail[...] = jnp.zeros_like(ctail)
        hcarry[...] = jnp.zeros_like(hcarry)

    proj_scr[...] = _dot(x.astype(BF16), win_ref[...])
    fill(2)

    lo = lax.broadcasted_iota(jnp.int32, (rows, LANES), 1) < A_HEAD_DIM

    def masked_variants(c0):
        ta = proj_scr[:, c0:c0 + LANES]
        tb = proj_scr[:, c0 + LANES:c0 + 2 * LANES]
        zero = jnp.zeros_like(ta)
        parts = [jnp.where(lo, ta, zero), jnp.where(lo, zero, tb),
                 jnp.where(lo, tb, zero), jnp.where(lo, zero, ta)]
        return jnp.concatenate(parts, axis=1).astype(BF16).reshape(bb, CHUNK, 4 * LANES)

    old_k = kbuf[:, CHUNK:A_BAND, :]
    kbuf[:, 0:A_BAND - CHUNK, :] = old_k
    kbuf[:, A_BAND - CHUNK:A_BAND, :] = masked_variants(AB_K0)
    old_v = vbuf[:, CHUNK:A_BAND, :]
    vbuf[:, 0:A_BAND - CHUNK, :] = old_v
    vbuf[:, A_BAND - CHUNK:A_BAND, :] = masked_variants(AB_V0)

    qi = lax.broadcasted_iota(jnp.int32, (2 * CHUNK, A_BAND), 0)
    ki = lax.broadcasted_iota(jnp.int32, (2 * CHUNK, A_BAND), 1)
    first_tile = qi < CHUNK
    dist = jnp.abs(jnp.where(first_tile, qi, qi - CHUNK) + (A_BAND - CHUNK) - ki).astype(F32)
    valid = ki >= (A_BAND - CHUNK) - CHUNK * n
    row_first = lax.broadcasted_iota(jnp.int32, (2 * CHUNK, 1), 0) < CHUNK
    scale = A_HEAD_DIM ** -0.5

    def head_consts(kv, parity):
        ha = 4 * kv + parity
        hb = ha + 2
        slope = jnp.where(row_first, 2.0 ** (-8.0 * (ha + 1) / A_HEADS), 2.0 ** (-8.0 * (hb + 1) / A_HEADS))
        sink = jnp.where(row_first, sinks_ref[ha], sinks_ref[hb])
        return slope * dist, sink

    consts = [[head_consts(kv, par) for par in range(2)] for kv in range(A_KV_HEADS)]

    def attend(it, carry):
        combos = [(it * group + sq, kv, par) for sq in range(group) for kv in range(A_KV_HEADS) for par in range(2)]
        idx = range(len(combos))
        rsl = [pl.ds(pl.multiple_of(b * CHUNK, CHUNK), CHUNK) for b, _, _ in combos]
        tile = [slice((2 * kv + par) * LANES, (2 * kv + par + 1) * LANES) for _, kv, par in combos]
        bias = [consts[kv][par][0] for _, kv, par in combos]
        sink = [consts[kv][par][1] for _, kv, par in combos]
        q2 = {}
        for i, (b, kv, par) in enumerate(combos):
            if par == 0:
                c0 = AB_Q0 + 2 * LANES * kv
                qq = jnp.concatenate([proj_scr[rsl[i], c0:c0 + LANES], proj_scr[rsl[i], c0 + LANES:c0 + 2 * LANES]],
                                     axis=0)
                q2[i] = q2[i + 1] = (qq * scale).astype(BF16)
        s = [_dot_nt(q2[i], kbuf[b, :, tile[i]]) for i, (b, _, _) in enumerate(combos)]
        if single_pass:
            fill(3)
        s = [jnp.where(valid, s[i] - bias[i], NEG) for i in idx]
        m = [jnp.maximum(jnp.max(s[i], axis=-1, keepdims=True), sink[i]) for i in idx]
        pr = [jnp.exp(s[i] - m[i]) for i in idx]
        den = [jnp.sum(pr[i], axis=-1, keepdims=True) + jnp.exp(sink[i] - m[i]) for i in idx]
        pv = [_dot(pr[i].astype(BF16), vbuf[b, :, tile[i]]) for i, (b, _, _) in enumerate(combos)]
        if single_pass:
            fill(1)
        for i, (b, kv, par) in enumerate(combos):
            if par == 0:
                c0 = AB_Q0 + 2 * LANES * kv
                acc = pv[i] / den[i] + pv[i + 1] / den[i + 1]
                y_scr[rsl[i], c0:c0 + LANES] = acc[0:CHUNK]
                y_scr[rsl[i], c0 + LANES:c0 + 2 * LANES] = acc[CHUNK:2 * CHUNK]
        return carry

    if single_pass:
        attend(0, 0)
    else:
        lax.fori_loop(0, bb // group, attend, 0)

    bx3 = proj_scr[:, AB_X0:AB_X0 + B_WIDTH].reshape(bb, CHUNK, B_WIDTH)
    conv = _causal_conv(bx3, ctail[...], convw_ref[...]) + convb_ref[...].reshape(1, 1, B_WIDTH)
    ctail[...] = bx3[:, CHUNK - SUBLANES:CHUNK, :]
    c2 = conv.reshape(rows, B_WIDTH)
    cb = c2.astype(BF16)
    r = jax.nn.sigmoid(_dot(cb, wa_ref[...]) + ba_ref[...])
    i = jax.nn.sigmoid(_dot(cb, wx_ref[...]) + bx_ref[...])
    fill(5)
    log_a = (-RG_C) * r * _softplus(-lam_ref[...])
    n_tiles = CHUNK // SUBLANES
    a = jnp.exp(log_a).reshape(bb * n_tiles, SUBLANES, B_WIDTH)
    u = (jnp.sqrt(_neg_expm1(2.0 * log_a)) * (i * c2)).reshape(bb * n_tiles, SUBLANES, B_WIDTH)
    ti = lax.broadcasted_iota(jnp.int32, (bb * n_tiles, SUBLANES, B_WIDTH), 1)
    step = 1
    while step < SUBLANES:
        keep = ti >= step
        a_prev = jnp.where(keep, pltpu.roll(a, step, 1), 1.0)
        u_prev = jnp.where(keep, pltpu.roll(u, step, 1), 0.0)
        u = u + a * u_prev
        a = a * a_prev
        step *= 2
    a = a.reshape(bb, n_tiles, SUBLANES, B_WIDTH)
    u = u.reshape(bb, n_tiles, SUBLANES, B_WIDTH)
    carry = hcarry[...].reshape(bb, 1, B_WIDTH)
    h_tiles = []
    for t in range(n_tiles):
        h_t = u[:, t] + a[:, t] * carry
        carry = h_t[:, SUBLANES - 1:SUBLANES, :]
        h_tiles.append(h_t)
    hcarry[...] = carry.reshape(bb, B_WIDTH)
    h = jnp.concatenate(h_tiles, axis=1)
    gate = jax.nn.gelu(proj_scr[:, AB_G0:AB_G0 + B_WIDTH], approximate=True)
    y_scr[:, A_WIDTH:A_WIDTH + B_WIDTH] = h.reshape(rows, B_WIDTH) * gate

    y = _dot(y_scr[...].astype(BF16), wout_ref[...])
    fill(len(side))
    return _layer_norm(DN_ALPHA * x + y, lng_ref[...], lnb_ref[...])


def _ffn_mixer_ab_body(sinks_ref, x_ref, wg_ref, wu_ref, wd_ref, lng1_ref, lnb1_ref, *rest, bb, group):
    weights, o_ref, h_scr, f1_scr, scratch = rest[:11], rest[11], rest[12], rest[13], rest[14:]
    j = pl.program_id(1)

    @pl.when(j == 0)
    def _():
        f1_scr[...] = jnp.zeros_like(f1_scr)

    prev = f1_scr[(j + 1) % 2]
    ffn = _ffn_steps(x_ref, wg_ref, wu_ref, wd_ref, lng1_ref, lnb1_ref, h_scr, f1_scr.at[j % 2])
    out = _mixer_ab_tile(j - 1, j <= 1, prev, sinks_ref, *weights, *scratch, bb=bb, group=group, side=ffn)
    o_ref[...] = out.reshape(o_ref.shape)


def _ab_scratch(bb):
    rows = bb * CHUNK
    return [
        pltpu.VMEM((rows, AB_COLS), F32),
        pltpu.VMEM((bb, A_BAND, 4 * LANES), BF16),
        pltpu.VMEM((bb, A_BAND, 4 * LANES), BF16),
        pltpu.VMEM((bb, SUBLANES, B_WIDTH), F32),
        pltpu.VMEM((bb, B_WIDTH), F32),
        pltpu.VMEM((rows, A_WIDTH + B_WIDTH), F32),
    ]


def _ffn_mixer_ab(x3, layer, wg, wu, wd, lng1, lnb1, win, sinks, convw, convb, wa, ba, wx, bx, lam, wout, lng, lnb):
    b, s, d = x3.shape
    f = wg.shape[-1]
    bb = min(MIX_BATCH, b)
    group = min(A_GROUP_SEQ, bb)
    n_chunks = s // CHUNK
    rows = bb * CHUNK
    assert b % bb == 0 and s % CHUNK == 0 and bb % group == 0 and rows % FFN_OUT_ROWS == 0 and f % FFN_COLS == 0
    in_tile = pl.BlockSpec((bb, CHUNK, d), lambda i, j: (i, jnp.minimum(j, n_chunks - 1), 0))
    out_tile = pl.BlockSpec((bb, CHUNK, d), lambda i, j: (i, jnp.maximum(j - 1, 0), 0))

    def of_layer(shape):
        return pl.BlockSpec((None,) + shape, lambda i, j: (layer,) + (0,) * len(shape), pipeline_mode=pl.Buffered(1))

    mixer_w = (win, convw, convb, wa, ba, wx, bx, lam, wout, lng, lnb)
    in_specs = ([pl.BlockSpec(memory_space=pltpu.SMEM), in_tile, of_layer((d, f)), of_layer((d, f)),
                 of_layer((f, d)), _resident(lng1.shape), _resident(lnb1.shape)]
                + [_resident(w.shape) for w in mixer_w])
    return pl.pallas_call(
        functools.partial(_ffn_mixer_ab_body, bb=bb, group=group),
        out_shape=jax.ShapeDtypeStruct((b, s, d), F32),
        grid=(b // bb, n_chunks + 1),
        in_specs=in_specs,
        out_specs=out_tile,
        scratch_shapes=[pltpu.VMEM((rows, f), BF16), pltpu.VMEM((2, rows, d), F32)] + _ab_scratch(bb),
        compiler_params=pltpu.CompilerParams(dimension_semantics=("arbitrary", "arbitrary"),
                                             vmem_limit_bytes=VMEM_LIMIT),
        name="ffn_mixer_ab",
    )(sinks, x3, wg, wu, wd, lng1, lnb1, *mixer_w)


C_GATE_COL = C_HEADS


def _mixer_c_body(x_ref, w1_ref, w2_ref, convw_ref, alog_ref, dtb_ref, ng_ref, wout_ref, lng_ref, lnb_ref,
                  o_ref, ctail, qkv_scr, z_scr, ba_scr, gcol_scr, grow_scr, state, og_scr, *, bb):
    n = pl.program_id(1)
    rows = bb * CHUNK
    d_model = x_ref.shape[-1]
    qkv_w = 3 * C_WIDTH

    @pl.when(n == 0)
    def _():
        ctail[...] = jnp.zeros_like(ctail)
        state[...] = jnp.zeros_like(state)

    x = x_ref[...].reshape(rows, d_model)
    xb = x.astype(BF16)
    n_pairs = C_HEADS // 2
    pair_w = 2 * C_HEAD_DIM

    def project_qkv(col0):
        csl = slice(col0, col0 + pair_w)
        cur = _dot(xb, w1_ref[:, csl]).reshape(bb, CHUNK, pair_w)
        conv = _causal_conv(cur, ctail[:, :, csl], convw_ref[:, csl])
        ctail[:, :, csl] = cur[:, CHUNK - SUBLANES:CHUNK, :]
        qkv_scr[:, csl] = _silu(conv).reshape(rows, pair_w)

    def project_z(p):
        z_scr[:, p * pair_w:(p + 1) * pair_w] = _dot(xb, w1_ref[:, qkv_w + p * pair_w:qkv_w + (p + 1) * pair_w])

    def pair_inputs(p):
        return [functools.partial(project_qkv, (3 * p + part) * pair_w) for part in range(3)] + [
            functools.partial(project_z, p)]

    ba_scr[...] = _dot(xb, w2_ref[...])
    for step in pair_inputs(0):
        step()

    ri = lax.broadcasted_iota(jnp.int32, (CHUNK, CHUNK), 0)
    ci = lax.broadcasted_iota(jnp.int32, (CHUNK, CHUNK), 1)
    tril = ri >= ci
    ri2 = lax.broadcasted_iota(jnp.int32, (CHUNK, LANES), 0)
    li2 = lax.broadcasted_iota(jnp.int32, (CHUNK, LANES), 1)
    ci2 = jnp.bitwise_and(li2, CHUNK - 1)
    lane_lo = li2 < CHUNK
    tril2 = ri2 >= ci2
    strict2 = ri2 > ci2
    eye2 = (ri2 == ci2).astype(F32)
    keep_lo = jnp.where(lane_lo, 1.0, 0.0).astype(BF16)
    keep_hi = jnp.where(lane_lo, 0.0, 1.0).astype(BF16)
    zero_t = jnp.zeros((CHUNK, C_HEAD_DIM), BF16)
    neg_rate = -jnp.exp(alog_ref[...])
    dtb = dtb_ref[...]
    norm_g = ng_ref[...]
    qscale = C_HEAD_DIM ** -0.5

    bab = ba_scr[...]
    g_all = neg_rate * _softplus(bab + dtb)
    ba_scr[...] = jax.nn.sigmoid(bab)
    g_hi = g_all.astype(BF16)
    rem = g_all - g_hi.astype(F32)
    g_mid = rem.astype(BF16)
    g_lo = (rem - g_mid.astype(F32)).astype(BF16)
    pieces = jnp.concatenate([part[b * CHUNK:(b + 1) * CHUNK] for part in (g_hi, g_mid, g_lo) for b in range(bb)],
                             axis=1)
    csum = _dot(tril.astype(BF16), pieces)
    for b in range(bb):
        gcb = (csum[:, b * LANES:(b + 1) * LANES] + csum[:, (bb + b) * LANES:(bb + b + 1) * LANES]
               + csum[:, (2 * bb + b) * LANES:(2 * bb + b + 1) * LANES])
        gcol_scr[b * CHUNK:(b + 1) * CHUNK, :] = gcb
        grow_scr[b * LANES:(b + 1) * LANES, :] = jnp.concatenate([gcb, gcb], axis=0).T

    def unit_norm(t, extra):
        return t * (lax.rsqrt(jnp.sum(t * t, axis=-1, keepdims=True) + NORM_EPS) * extra)

    def head_cols(part, h):
        c0 = (3 * (h // 2) + part) * pair_w + (h % 2) * C_HEAD_DIM
        return slice(c0, c0 + C_HEAD_DIM)

    def out_cols(h):
        return slice(h * C_HEAD_DIM, (h + 1) * C_HEAD_DIM)

    def lane_cat(a, b):
        return jnp.concatenate([a, b], axis=1)

    def block_diag2(top, bottom):
        return jnp.concatenate([lane_cat(top, jnp.zeros_like(top)), lane_cat(jnp.zeros_like(bottom), bottom)], axis=0)

    def delta_rule_pair(pair, side):
        side = list(side)

        def fill(count):
            for _ in range(min(count, len(side))):
                side.pop(0)()

        units = [(s, pair) for s in range(bb)]
        idx = range(len(units))
        rsl = [slice(b * CHUNK, (b + 1) * CHUNK) for b, _ in units]
        heads = [(2 * p, 2 * p + 1) for _, p in units]
        st = [[state[b * C_HEADS + h] for h in heads[i]] for i, (b, _) in enumerate(units)]
        beta = [[ba_scr[rsl[i], h:h + 1] for h in heads[i]] for i in idx]
        gc = [[gcol_scr[rsl[i], C_GATE_COL + h:C_GATE_COL + h + 1] for h in heads[i]] for i in idx]
        gr = [[grow_scr[b * LANES + C_GATE_COL + h:b * LANES + C_GATE_COL + h + 1, :] for h in heads[i]]
              for i, (b, _) in enumerate(units)]
        gl = [[gc[i][j][CHUNK - 1:CHUNK, :] for j in range(2)] for i in idx]
        eg = [[jnp.exp(gc[i][j]) for j in range(2)] for i in idx]
        q = [[unit_norm(qkv_scr[rsl[i], head_cols(0, h)], qscale) for h in heads[i]] for i in idx]
        k = [[unit_norm(qkv_scr[rsl[i], head_cols(1, h)], 1.0) for h in heads[i]] for i in idx]
        v = [[qkv_scr[rsl[i], head_cols(2, h)] for h in heads[i]] for i in idx]
        kb = [[k[i][j] * beta[i][j] for j in range(2)] for i in idx]
        lhs = [jnp.concatenate([lane_cat(q[i][0], q[i][1]), lane_cat(kb[i][0], kb[i][1])], axis=0).astype(BF16)
               for i in idx]
        kbd = [block_diag2(k[i][0].astype(BF16), k[i][1].astype(BF16)) for i in idx]
        qk = [_dot_nt(lhs[i], kbd[i]) for i in idx]
        fill(1)
        gc2 = [jnp.where(lane_lo, gc[i][0], gc[i][1]) for i in idx]
        gr2 = [jnp.where(lane_lo[0:1], gr[i][0], gr[i][1]) for i in idx]
        decay = [jnp.where(tril2, jnp.exp(jnp.where(tril2, gc2[i] - gr2[i], 0.0)), 0.0) for i in idx]
        attn = [(qk[i][0:CHUNK] * decay[i]).astype(BF16) for i in idx]
        lmat = [jnp.where(strict2, qk[i][CHUNK:2 * CHUNK] * decay[i], 0.0) for i in idx]
        tinv = [eye2 - lmat[i] for i in idx]
        pw = [lmat[i].astype(BF16) for i in idx]
        for level in range(5):
            pw = [_dot(pw[i], jnp.concatenate([pw[i] * keep_lo, pw[i] * keep_hi], axis=0)).astype(BF16) for i in idx]
            tinv = [tinv[i] + _dot(tinv[i].astype(BF16), jnp.concatenate([pw[i] * keep_lo, pw[i] * keep_hi], axis=0))
                    for i in idx]
            if level % 2 == 0:
                fill(1)
        rhs = [jnp.concatenate([
            jnp.concatenate([(v[i][0] * beta[i][0]).astype(BF16), zero_t, (kb[i][0] * eg[i][0]).astype(BF16), zero_t], axis=1),
            jnp.concatenate([zero_t, (v[i][1] * beta[i][1]).astype(BF16), zero_t, (kb[i][1] * eg[i][1]).astype(BF16)], axis=1),
        ], axis=0) for i in idx]
        uw = [_dot(tinv[i].astype(BF16), rhs[i]) for i in idx]
        sbd = [block_diag2(st[i][0].astype(BF16), st[i][1].astype(BF16)) for i in idx]
        wq = [_dot(jnp.concatenate([uw[i][:, 2 * C_HEAD_DIM:],
                                    lane_cat(q[i][0] * eg[i][0], q[i][1] * eg[i][1])], axis=0).astype(BF16), sbd[i])
              for i in idx]
        vnb = [(uw[i][:, 0:2 * C_HEAD_DIM] - wq[i][0:CHUNK]).astype(BF16) for i in idx]
        vbd = [block_diag2(vnb[i][:, 0:C_HEAD_DIM], vnb[i][:, C_HEAD_DIM:]) for i in idx]
        o = [wq[i][CHUNK:2 * CHUNK] + _dot(attn[i], vbd[i]) for i in idx]
        fill(len(side))
        for i, (b, _) in enumerate(units):
            for j, h in enumerate(heads[i]):
                hs = slice(j * C_HEAD_DIM, (j + 1) * C_HEAD_DIM)
                kdec = (k[i][j] * jnp.exp(gl[i][j] - gc[i][j])).astype(BF16)
                state[b * C_HEADS + h] = st[i][j] * jnp.exp(gl[i][j]) + _dot_tn(kdec, vnb[i][:, hs])
                oh = o[i][:, hs]
                on = oh * lax.rsqrt(jnp.mean(oh * oh, axis=-1, keepdims=True) + NORM_EPS) * norm_g
                og_scr[rsl[i], out_cols(h)] = on * _silu(z_scr[rsl[i], out_cols(h)])

    for p in range(n_pairs):
        delta_rule_pair(p, pair_inputs(p + 1) if p + 1 < n_pairs else [])

    y = _dot(og_scr[...].astype(BF16), wout_ref[...])
    out = _layer_norm(DN_ALPHA * x + y, lng_ref[...], lnb_ref[...])
    o_ref[...] = out.reshape(bb, CHUNK, d_model)


def _mixer_c(x3, w1, w2, convw, alog, dtb, ng, wout, lng, lnb):
    b, s, d = x3.shape
    bb = min(MIX_BATCH, b)
    assert b % bb == 0 and s % CHUNK == 0
    rows = bb * CHUNK
    tile = pl.BlockSpec((bb, CHUNK, d), lambda i, j: (i, j, 0))
    in_specs = [tile] + [_resident(a.shape) for a in (w1, w2, convw, alog, dtb, ng, wout, lng, lnb)]
    return pl.pallas_call(
        functools.partial(_mixer_c_body, bb=bb),
        out_shape=jax.ShapeDtypeStruct((b, s, d), F32),
        grid=(b // bb, s // CHUNK),
        in_specs=in_specs,
        out_specs=tile,
        scratch_shapes=[
            pltpu.VMEM((bb, SUBLANES, 3 * C_WIDTH), F32),
            pltpu.VMEM((rows, 3 * C_WIDTH), F32),
            pltpu.VMEM((rows, C_WIDTH), F32),
            pltpu.VMEM((rows, LANES), F32),
            pltpu.VMEM((rows, LANES), F32),
            pltpu.VMEM((bb * LANES, LANES), F32),
            pltpu.VMEM((bb * C_HEADS, C_HEAD_DIM, C_HEAD_DIM), F32),
            pltpu.VMEM((rows, C_WIDTH), F32),
        ],
        compiler_params=pltpu.CompilerParams(dimension_semantics=("arbitrary", "arbitrary"),
                                             vmem_limit_bytes=VMEM_LIMIT),
        name="mixer_c",
    )(x3, w1, w2, convw, alog, dtb, ng, wout, lng, lnb)


def _block_diag(w):
    nb, bs, _ = w.shape
    out = jnp.zeros((nb * bs, nb * bs), w.dtype)
    for i in range(nb):
        out = out.at[i * bs:(i + 1) * bs, i * bs:(i + 1) * bs].set(w[i])
    return out


def _pair_major(m):
    pw = 2 * C_HEAD_DIM
    return jnp.concatenate([m[..., part * C_WIDTH + p * pw:part * C_WIDTH + (p + 1) * pw]
                            for p in range(C_HEADS // 2) for part in range(3)], axis=-1)


def _row(v):
    return v.reshape(1, -1).astype(F32)


def kernel(x, p, ffn1_wg, ffn1_wu, ffn1_wd, ffn2_wg, ffn2_wu, ffn2_wd, ln_g, ln_b, ple_wg, ple_bg, ple_wp, ab_w_in, a_sinks, b_conv_w, b_conv_b, b_wa, b_ba, b_wx, b_bx, b_lam, ab_w_out, c_w_in, c_conv_w, c_a_log, c_dt_bias, c_norm_g, c_w_out):
    b, s, d = x.shape
    n = b * s
    ffn1 = [w.astype(BF16) for w in (ffn1_wg, ffn1_wu, ffn1_wd)]
    ffn2 = [w.astype(BF16) for w in (ffn2_wg, ffn2_wu, ffn2_wd)]
    ple_w = (ple_wg.astype(BF16), ple_wp.astype(BF16))
    p_all = p.reshape(DEPTH, n, -1)
    for i in range(DEPTH):
        j = i // 2
        if i % 2 == 0:
            w = ab_w_in[j]
            o1, o2, o3 = A_WIDTH, A_WIDTH + A_KV_WIDTH, A_WIDTH + 2 * A_KV_WIDTH
            hd = A_HEAD_DIM
            swap = lambda m: jnp.concatenate([m[:, hd:2 * hd], m[:, 0:hd]], axis=1)
            win = jnp.concatenate([w[:, :o1], w[:, o1:o2], swap(w[:, o1:o2]), w[:, o2:o3], swap(w[:, o2:o3]),
                                   w[:, o3:]], axis=1).astype(BF16)
            x = _ffn_mixer_ab(x, i, *ffn1, _row(ln_g[i, 0]), _row(ln_b[i, 0]),
                              win, a_sinks[j].astype(F32), b_conv_w[j], _row(b_conv_b[j]),
                              _block_diag(b_wa[j]).astype(BF16), _row(b_ba[j]),
                              _block_diag(b_wx[j]).astype(BF16), _row(b_bx[j]), _row(b_lam[j]),
                              ab_w_out[j].astype(BF16), _row(ln_g[i, 1]), _row(ln_b[i, 1]))
        else:
            x = _ffn(x.reshape(n, d), i, *ffn1, _row(ln_g[i, 0]), _row(ln_b[i, 0])).reshape(b, s, d)
            w = c_w_in[j]
            w1 = jnp.concatenate([_pair_major(w[:, :3 * C_WIDTH]), w[:, 3 * C_WIDTH:4 * C_WIDTH]], axis=1).astype(BF16)
            w2 = jnp.pad(w[:, 4 * C_WIDTH:], ((0, 0), (0, LANES - 2 * C_HEADS))).astype(BF16)
            pad_gate = lambda v: jnp.pad(v.astype(F32), (C_GATE_COL, LANES - C_GATE_COL - C_HEADS)).reshape(1, LANES)
            x = _mixer_c(x, w1, w2, _pair_major(c_conv_w[j]), pad_gate(c_a_log[j]), pad_gate(c_dt_bias[j]),
                         _row(c_norm_g[j]), c_w_out[j].astype(BF16), _row(ln_g[i, 1]), _row(ln_b[i, 1]))
        x = _ffn(x.reshape(n, d), i, *ffn2, _row(ln_g[i, 2]), _row(ln_b[i, 2]),
                 ple_args=(p_all, ple_w[0], _row(ple_bg[i]), ple_w[1])).reshape(b, s, d)
    return x
```

```python
import functools

import jax
import jax.numpy as jnp
from jax import lax
from jax.experimental import pallas as pl
from jax.experimental.pallas import tpu as pltpu

F32 = jnp.float32
BF16 = jnp.bfloat16

DEPTH = 2
CHUNK = 64
A_HEADS = 8
A_KV_HEADS = 2
A_HEAD_DIM = 64
A_WIDTH = A_HEADS * A_HEAD_DIM
A_KV_WIDTH = A_KV_HEADS * A_HEAD_DIM
A_PREV_CHUNKS = 2
A_BAND = (A_PREV_CHUNKS + 1) * CHUNK
B_WIDTH = 512
B_BLOCKS = 8
B_CONV = 4
RG_C = 8.0
C_HEADS = 8
C_HEAD_DIM = 128
C_WIDTH = C_HEADS * C_HEAD_DIM
C_CONV = 4
DN_ALPHA = (2.0 * DEPTH) ** 0.25
LN_EPS = 1e-5
NORM_EPS = 1e-6
NEG = -1e30

LANES = 128
SUBLANES = 8
FFN_ROWS = 1024
FFN_COLS = 256
FFN_OUT_ROWS = 256
MIX_BATCH = 8
A_GROUP_SEQ = 8
C_PHASE_PAIRS = 2
VMEM_LIMIT = 56 * 1024 * 1024


def _dot(a, b):
    return jnp.dot(a, b, preferred_element_type=F32)


def _dot_nt(a, b):
    return lax.dot_general(a, b, (((1,), (1,)), ((), ())), preferred_element_type=F32)


def _dot_tn(a, b):
    return lax.dot_general(a, b, (((0,), (0,)), ((), ())), preferred_element_type=F32)


def _layer_norm(z, g, b):
    mu = jnp.mean(z, axis=-1, keepdims=True)
    d = z - mu
    var = jnp.mean(d * d, axis=-1, keepdims=True)
    return d * lax.rsqrt(var + LN_EPS) * g + b


def _silu(x):
    return x * jax.nn.sigmoid(x)


def _softplus(x):
    e = jnp.exp(-jnp.abs(x))
    u = 1.0 + e
    log1p_e = jnp.where(u == 1.0, e, jnp.log(u) * (e / (u - 1.0)))
    return jnp.maximum(x, 0.0) + log1p_e


def _neg_expm1(y):
    return -jnp.tanh(0.5 * y) * (jnp.exp(y) + 1.0)


def _causal_conv(cur, tail, w):
    taps = w.shape[0]
    b, t, c = cur.shape
    nt = t // SUBLANES
    tiles = jnp.concatenate([tail, cur], axis=1).reshape(b * (nt + 1), SUBLANES, c)
    sub = lax.broadcasted_iota(jnp.int32, (b, nt, SUBLANES, c), 2)
    out = w[taps - 1:taps].reshape(1, 1, -1) * cur
    for d in range(1, taps):
        rot = pltpu.roll(tiles, d, 1).reshape(b, nt + 1, SUBLANES, c)
        shifted = jnp.where(sub < d, rot[:, 0:nt], rot[:, 1:nt + 1]).reshape(b, t, c)
        out = out + w[taps - 1 - d:taps - d].reshape(1, 1, -1) * shifted
    return out


def _resident(shape):
    zeros = (0,) * len(shape)
    return pl.BlockSpec(shape, lambda *_: zeros, pipeline_mode=pl.Buffered(1))


def _tile_rows(ref, start, stop):
    if len(ref.shape) == 2:
        return ref[start:stop, :]
    return ref[start // CHUNK:stop // CHUNK].reshape(stop - start, ref.shape[-1])


def _ffn_steps(x_ref, wg_ref, wu_ref, wd_ref, lng_ref, lnb_ref, h_scr, out_ref):
    rows = h_scr.shape[0]
    cache = {}

    def xb():
        if "xb" not in cache:
            cache["xb"] = _tile_rows(x_ref, 0, rows).astype(BF16)
        return cache["xb"]

    def up(c):
        sl = slice(c * FFN_COLS, (c + 1) * FFN_COLS)
        g = _dot(xb(), wg_ref[:, sl])
        u = _dot(xb(), wu_ref[:, sl])
        h_scr[:, sl] = (_silu(g) * u).astype(BF16)

    def down(r):
        r0, r1 = r * FFN_OUT_ROWS, (r + 1) * FFN_OUT_ROWS
        y = _dot(h_scr[r0:r1, :], wd_ref[...])
        out_ref[r0:r1, :] = _layer_norm(DN_ALPHA * _tile_rows(x_ref, r0, r1) + 0.5 * y, lng_ref[...], lnb_ref[...])

    return ([functools.partial(up, c) for c in range(wg_ref.shape[-1] // FFN_COLS)]
            + [functools.partial(down, r) for r in range(rows // FFN_OUT_ROWS)])


def _ffn_rows(*refs):
    for step in _ffn_steps(*refs):
        step()


def _ffn_body(x_ref, wg_ref, wu_ref, wd_ref, lng_ref, lnb_ref, *rest, ple):
    if ple:
        p_ref, pwg_ref, pbg_ref, pwp_ref, o_ref, h_scr = rest
    else:
        o_ref, h_scr = rest
    _ffn_rows(x_ref, wg_ref, wu_ref, wd_ref, lng_ref, lnb_ref, h_scr, o_ref)
    if ple:
        out = o_ref[...]
        gate = jax.nn.sigmoid(_dot(out.astype(BF16), pwg_ref[...]) + pbg_ref[...])
        o_ref[...] = out + gate * _dot(p_ref[...].astype(BF16), pwp_ref[...])


def _ffn(x2, layer, wg, wu, wd, lng, lnb, ple_args=None):
    n, d = x2.shape
    f = wg.shape[-1]
    rows = min(FFN_ROWS, n)
    assert n % rows == 0 and f % FFN_COLS == 0 and rows % FFN_OUT_ROWS == 0
    row_spec = pl.BlockSpec((rows, d), lambda i: (i, 0))

    def of_layer(shape):
        return pl.BlockSpec((None,) + shape, lambda i: (layer,) + (0,) * len(shape), pipeline_mode=pl.Buffered(1))

    in_specs = [row_spec, of_layer((d, f)), of_layer((d, f)), of_layer((f, d)),
                _resident((1, d)), _resident((1, d))]
    args = [x2, wg, wu, wd, lng, lnb]
    if ple_args is not None:
        p_all, pwg, pbg, pwp = ple_args
        dp = p_all.shape[-1]
        in_specs += [pl.BlockSpec((None, rows, dp), lambda i: (layer, i, 0)), of_layer((d, d)),
                     _resident((1, d)), of_layer((dp, d))]
        args += [p_all, pwg, pbg, pwp]
    return pl.pallas_call(
        functools.partial(_ffn_body, ple=ple_args is not None),
        out_shape=jax.ShapeDtypeStruct((n, d), F32),
        grid=(n // rows,),
        in_specs=in_specs,
        out_specs=row_spec,
        scratch_shapes=[pltpu.VMEM((rows, f), BF16)],
        compiler_params=pltpu.CompilerParams(dimension_semantics=("arbitrary",),
                                             vmem_limit_bytes=VMEM_LIMIT),
        name="ffn_ple" if ple_args is not None else "ffn",
    )(*args)


AB_Q0 = 0
AB_K0 = A_WIDTH
AB_V0 = AB_K0 + 2 * A_KV_WIDTH
AB_X0 = AB_V0 + 2 * A_KV_WIDTH
AB_G0 = AB_X0 + B_WIDTH
AB_COLS = AB_G0 + B_WIDTH


def _mixer_ab_tile(n, reset, x, sinks_ref, win_ref, convw_ref, convb_ref, wa_ref, ba_ref, wx_ref, bx_ref,
                   lam_ref, wout_ref, lng_ref, lnb_ref, proj_scr, kbuf, vbuf, ctail, hcarry, y_scr, *, bb, group,
                   side=()):
    rows = bb * CHUNK
    single_pass = bb == group
    side = list(side)

    def fill(count):
        for _ in range(min(count, len(side))):
            side.pop(0)()

    @pl.when(reset)
    def _():
        kbuf[...] = jnp.zeros_like(kbuf)
        vbuf[...] = jnp.zeros_like(vbuf)
        ctail[...] = jnp.zeros_like(ctail)
        hcarry[...] = jnp.zeros_like(hcarry)

    proj_scr[...] = _dot(x.astype(BF16), win_ref[...])
    fill(2)

    lo = lax.broadcasted_iota(jnp.int32, (rows, LANES), 1) < A_HEAD_DIM

    def masked_variants(c0):
        ta = proj_scr[:, c0:c0 + LANES]
        tb = proj_scr[:, c0 + LANES:c0 + 2 * LANES]
        zero = jnp.zeros_like(ta)
        parts = [jnp.where(lo, ta, zero), jnp.where(lo, zero, tb),
                 jnp.where(lo, tb, zero), jnp.where(lo, zero, ta)]
        return jnp.concatenate(parts, axis=1).astype(BF16).reshape(bb, CHUNK, 4 * LANES)

    old_k = kbuf[:, CHUNK:A_BAND, :]
    kbuf[:, 0:A_BAND - CHUNK, :] = old_k
    kbuf[:, A_BAND - CHUNK:A_BAND, :] = masked_variants(AB_K0)
    old_v = vbuf[:, CHUNK:A_BAND, :]
    vbuf[:, 0:A_BAND - CHUNK, :] = old_v
    vbuf[:, A_BAND - CHUNK:A_BAND, :] = masked_variants(AB_V0)

    qi = lax.broadcasted_iota(jnp.int32, (2 * CHUNK, A_BAND), 0)
    ki = lax.broadcasted_iota(jnp.int32, (2 * CHUNK, A_BAND), 1)
    first_tile = qi < CHUNK
    dist = jnp.abs(jnp.where(first_tile, qi, qi - CHUNK) + (A_BAND - CHUNK) - ki).astype(F32)
    valid = ki >= (A_BAND - CHUNK) - CHUNK * n
    row_first = lax.broadcasted_iota(jnp.int32, (2 * CHUNK, 1), 0) < CHUNK
    scale = A_HEAD_DIM ** -0.5

    def head_consts(kv, parity):
        ha = 4 * kv + parity
        hb = ha + 2
        slope = jnp.where(row_first, 2.0 ** (-8.0 * (ha + 1) / A_HEADS), 2.0 ** (-8.0 * (hb + 1) / A_HEADS))
        sink = jnp.where(row_first, sinks_ref[ha], sinks_ref[hb])
        return slope * dist, sink

    consts = [[head_consts(kv, par) for par in range(2)] for kv in range(A_KV_HEADS)]

    def attend(it, carry):
        combos = [(it * group + sq, kv, par) for sq in range(group) for kv in range(A_KV_HEADS) for par in range(2)]
        idx = range(len(combos))
        rsl = [pl.ds(pl.multiple_of(b * CHUNK, CHUNK), CHUNK) for b, _, _ in combos]
        tile = [slice((2 * kv + par) * LANES, (2 * kv + par + 1) * LANES) for _, kv, par in combos]
        bias = [consts[kv][par][0] for _, kv, par in combos]
        sink = [consts[kv][par][1] for _, kv, par in combos]
        q2 = {}
        for i, (b, kv, par) in enumerate(combos):
            if par == 0:
                c0 = AB_Q0 + 2 * LANES * kv
                qq = jnp.concatenate([proj_scr[rsl[i], c0:c0 + LANES], proj_scr[rsl[i], c0 + LANES:c0 + 2 * LANES]],
                                     axis=0)
                q2[i] = q2[i + 1] = (qq * scale).astype(BF16)
        s = [_dot_nt(q2[i], kbuf[b, :, tile[i]]) for i, (b, _, _) in enumerate(combos)]
        if single_pass:
            fill(3)
        s = [jnp.where(valid, s[i] - bias[i], NEG) for i in idx]
        m = [jnp.maximum(jnp.max(s[i], axis=-1, keepdims=True), sink[i]) for i in idx]
        pr = [jnp.exp(s[i] - m[i]) for i in idx]
        den = [jnp.sum(pr[i], axis=-1, keepdims=True) + jnp.exp(sink[i] - m[i]) for i in idx]
        pv = [_dot(pr[i].astype(BF16), vbuf[b, :, tile[i]]) for i, (b, _, _) in enumerate(combos)]
        if single_pass:
            fill(1)
        for i, (b, kv, par) in enumerate(combos):
            if par == 0:
                c0 = AB_Q0 + 2 * LANES * kv
                acc = pv[i] / den[i] + pv[i + 1] / den[i + 1]
                y_scr[rsl[i], c0:c0 + LANES] = acc[0:CHUNK]
                y_scr[rsl[i], c0 + LANES:c0 + 2 * LANES] = acc[CHUNK:2 * CHUNK]
        return carry

    if single_pass:
        attend(0, 0)
    else:
        lax.fori_loop(0, bb // group, attend, 0)

    bx3 = proj_scr[:, AB_X0:AB_X0 + B_WIDTH].reshape(bb, CHUNK, B_WIDTH)
    conv = _causal_conv(bx3, ctail[...], convw_ref[...]) + convb_ref[...].reshape(1, 1, B_WIDTH)
    ctail[...] = bx3[:, CHUNK - SUBLANES:CHUNK, :]
    c2 = conv.reshape(rows, B_WIDTH)
    cb = c2.astype(BF16)
    r = jax.nn.sigmoid(_dot(cb, wa_ref[...]) + ba_ref[...])
    i = jax.nn.sigmoid(_dot(cb, wx_ref[...]) + bx_ref[...])
    fill(5)
    log_a = (-RG_C) * r * _softplus(-lam_ref[...])
    n_tiles = CHUNK // SUBLANES
    a = jnp.exp(log_a).reshape(bb * n_tiles, SUBLANES, B_WIDTH)
    u = (jnp.sqrt(_neg_expm1(2.0 * log_a)) * (i * c2)).reshape(bb * n_tiles, SUBLANES, B_WIDTH)
    ti = lax.broadcasted_iota(jnp.int32, (bb * n_tiles, SUBLANES, B_WIDTH), 1)
    step = 1
    while step < SUBLANES:
        keep = ti >= step
        a_prev = jnp.where(keep, pltpu.roll(a, step, 1), 1.0)
        u_prev = jnp.where(keep, pltpu.roll(u, step, 1), 0.0)
        u = u + a * u_prev
        a = a * a_prev
        step *= 2
    a = a.reshape(bb, n_tiles, SUBLANES, B_WIDTH)
    u = u.reshape(bb, n_tiles, SUBLANES, B_WIDTH)
    carry = hcarry[...].reshape(bb, 1, B_WIDTH)
    h_tiles = []
    for t in range(n_tiles):
        h_t = u[:, t] + a[:, t] * carry
        carry = h_t[:, SUBLANES - 1:SUBLANES, :]
        h_tiles.append(h_t)
    hcarry[...] = carry.reshape(bb, B_WIDTH)
    h = jnp.concatenate(h_tiles, axis=1)
    gate = jax.nn.gelu(proj_scr[:, AB_G0:AB_G0 + B_WIDTH], approximate=True)
    y_scr[:, A_WIDTH:A_WIDTH + B_WIDTH] = h.reshape(rows, B_WIDTH) * gate

    y = _dot(y_scr[...].astype(BF16), wout_ref[...])
    fill(len(side))
    return _layer_norm(DN_ALPHA * x + y, lng_ref[...], lnb_ref[...])


def _ffn_mixer_ab_body(sinks_ref, x_ref, wg_ref, wu_ref, wd_ref, lng1_ref, lnb1_ref, *rest, bb, group):
    weights, o_ref, h_scr, f1_scr, scratch = rest[:11], rest[11], rest[12], rest[13], rest[14:]
    j = pl.program_id(1)

    @pl.when(j == 0)
    def _():
        f1_scr[...] = jnp.zeros_like(f1_scr)

    prev = f1_scr[(j + 1) % 2]
    ffn = _ffn_steps(x_ref, wg_ref, wu_ref, wd_ref, lng1_ref, lnb1_ref, h_scr, f1_scr.at[j % 2])
    out = _mixer_ab_tile(j - 1, j <= 1, prev, sinks_ref, *weights, *scratch, bb=bb, group=group, side=ffn)
    o_ref[...] = out.reshape(o_ref.shape)


def _ab_scratch(bb):
    rows = bb * CHUNK
    return [
        pltpu.VMEM((rows, AB_COLS), F32),
        pltpu.VMEM((bb, A_BAND, 4 * LANES), BF16),
        pltpu.VMEM((bb, A_BAND, 4 * LANES), BF16),
        pltpu.VMEM((bb, SUBLANES, B_WIDTH), F32),
        pltpu.VMEM((bb, B_WIDTH), F32),
        pltpu.VMEM((rows, A_WIDTH + B_WIDTH), F32),
    ]


def _ffn_mixer_ab(x3, layer, wg, wu, wd, lng1, lnb1, win, sinks, convw, convb, wa, ba, wx, bx, lam, wout, lng, lnb):
    b, s, d = x3.shape
    f = wg.shape[-1]
    bb = min(MIX_BATCH, b)
    group = min(A_GROUP_SEQ, bb)
    n_chunks = s // CHUNK
    rows = bb * CHUNK
    assert b % bb == 0 and s % CHUNK == 0 and bb % group == 0 and rows % FFN_OUT_ROWS == 0 and f % FFN_COLS == 0
    in_tile = pl.BlockSpec((bb, CHUNK, d), lambda i, j: (i, jnp.minimum(j, n_chunks - 1), 0))
    out_tile = pl.BlockSpec((bb, CHUNK, d), lambda i, j: (i, jnp.maximum(j - 1, 0), 0))

    def of_layer(shape):
        return pl.BlockSpec((None,) + shape, lambda i, j: (layer,) + (0,) * len(shape), pipeline_mode=pl.Buffered(1))

    mixer_w = (win, convw, convb, wa, ba, wx, bx, lam, wout, lng, lnb)
    in_specs = ([pl.BlockSpec(memory_space=pltpu.SMEM), in_tile, of_layer((d, f)), of_layer((d, f)),
                 of_layer((f, d)), _resident(lng1.shape), _resident(lnb1.shape)]
                + [_resident(w.shape) for w in mixer_w])
    return pl.pallas_call(
        functools.partial(_ffn_mixer_ab_body, bb=bb, group=group),
        out_shape=jax.ShapeDtypeStruct((b, s, d), F32),
        grid=(b // bb, n_chunks + 1),
        in_specs=in_specs,
        out_specs=out_tile,
        scratch_shapes=[pltpu.VMEM((rows, f), BF16), pltpu.VMEM((2, rows, d), F32)] + _ab_scratch(bb),
        compiler_params=pltpu.CompilerParams(dimension_semantics=("arbitrary", "arbitrary"),
                                             vmem_limit_bytes=VMEM_LIMIT),
        name="ffn_mixer_ab",
    )(sinks, x3, wg, wu, wd, lng1, lnb1, *mixer_w)


C_GATE_COL = C_HEADS


def _mixer_c_body(x_ref, w1_ref, w2_ref, convw_ref, alog_ref, dtb_ref, ng_ref, wout_ref, lng_ref, lnb_ref,
                  o_ref, ctail, qkv_scr, z_scr, ba_scr, gcol_scr, grow_scr, state, og_scr, *, bb):
    n = pl.program_id(1)
    rows = bb * CHUNK
    d_model = x_ref.shape[-1]
    qkv_w = 3 * C_WIDTH

    @pl.when(n == 0)
    def _():
        ctail[...] = jnp.zeros_like(ctail)
        state[...] = jnp.zeros_like(state)

    x = x_ref[...].reshape(rows, d_model)
    xb = x.astype(BF16)
    n_pairs = C_HEADS // 2
    pair_w = 2 * C_HEAD_DIM

    def project_qkv(col0):
        csl = slice(col0, col0 + pair_w)
        cur = _dot(xb, w1_ref[:, csl]).reshape(bb, CHUNK, pair_w)
        conv = _causal_conv(cur, ctail[:, :, csl], convw_ref[:, csl])
        ctail[:, :, csl] = cur[:, CHUNK - SUBLANES:CHUNK, :]
        qkv_scr[:, csl] = _silu(conv).reshape(rows, pair_w)

    def project_z(p):
        z_scr[:, p * pair_w:(p + 1) * pair_w] = _dot(xb, w1_ref[:, qkv_w + p * pair_w:qkv_w + (p + 1) * pair_w])

    def pair_inputs(p):
        return [functools.partial(project_qkv, (3 * p + part) * pair_w) for part in range(3)] + [
            functools.partial(project_z, p)]

    ba_scr[...] = _dot(xb, w2_ref[...])
    for p in range(C_PHASE_PAIRS):
        for step in pair_inputs(p):
            step()

    ri = lax.broadcasted_iota(jnp.int32, (CHUNK, CHUNK), 0)
    ci = lax.broadcasted_iota(jnp.int32, (CHUNK, CHUNK), 1)
    tril = ri >= ci
    ri2 = lax.broadcasted_iota(jnp.int32, (CHUNK, LANES), 0)
    li2 = lax.broadcasted_iota(jnp.int32, (CHUNK, LANES), 1)
    ci2 = jnp.bitwise_and(li2, CHUNK - 1)
    lane_lo = li2 < CHUNK
    tril2 = ri2 >= ci2
    strict2 = ri2 > ci2
    eye2 = (ri2 == ci2).astype(F32)
    keep_lo = jnp.where(lane_lo, 1.0, 0.0).astype(BF16)
    keep_hi = jnp.where(lane_lo, 0.0, 1.0).astype(BF16)
    zero_t = jnp.zeros((CHUNK, C_HEAD_DIM), BF16)
    neg_rate = -jnp.exp(alog_ref[...])
    dtb = dtb_ref[...]
    norm_g = ng_ref[...]
    qscale = C_HEAD_DIM ** -0.5

    bab = ba_scr[...]
    g_all = neg_rate * _softplus(bab + dtb)
    ba_scr[...] = jax.nn.sigmoid(bab)
    g_hi = g_all.astype(BF16)
    rem = g_all - g_hi.astype(F32)
    g_mid = rem.astype(BF16)
    g_lo = (rem - g_mid.astype(F32)).astype(BF16)
    pieces = jnp.concatenate([part[b * CHUNK:(b + 1) * CHUNK] for part in (g_hi, g_mid, g_lo) for b in range(bb)],
                             axis=1)
    csum = _dot(tril.astype(BF16), pieces)
    for b in range(bb):
        gcb = (csum[:, b * LANES:(b + 1) * LANES] + csum[:, (bb + b) * LANES:(bb + b + 1) * LANES]
               + csum[:, (2 * bb + b) * LANES:(2 * bb + b + 1) * LANES])
        gcol_scr[b * CHUNK:(b + 1) * CHUNK, :] = gcb
        grow_scr[b * LANES:(b + 1) * LANES, :] = jnp.concatenate([gcb, gcb], axis=0).T

    def unit_norm(t, extra):
        return t * (lax.rsqrt(jnp.sum(t * t, axis=-1, keepdims=True) + NORM_EPS) * extra)

    def head_cols(part, h):
        c0 = (3 * (h // 2) + part) * pair_w + (h % 2) * C_HEAD_DIM
        return slice(c0, c0 + C_HEAD_DIM)

    def out_cols(h):
        return slice(h * C_HEAD_DIM, (h + 1) * C_HEAD_DIM)

    def lane_cat(a, b):
        return jnp.concatenate([a, b], axis=1)

    def block_diag2(top, bottom):
        return jnp.concatenate([lane_cat(top, jnp.zeros_like(top)), lane_cat(jnp.zeros_like(bottom), bottom)], axis=0)

    def delta_rule_pairs(pairs, side):
        side = list(side)
        fill_points = [8]

        def fill():
            for _ in range(-(-len(side) // fill_points[0])):
                side.pop(0)()
            fill_points[0] -= 1

        units = [(s, pair) for pair in pairs for s in range(bb)]
        idx = range(len(units))
        rsl = [slice(b * CHUNK, (b + 1) * CHUNK) for b, _ in units]
        heads = [(2 * p, 2 * p + 1) for _, p in units]
        st = [[state[b * C_HEADS + h] for h in heads[i]] for i, (b, _) in enumerate(units)]
        beta = [[ba_scr[rsl[i], h:h + 1] for h in heads[i]] for i in idx]
        gc = [[gcol_scr[rsl[i], C_GATE_COL + h:C_GATE_COL + h + 1] for h in heads[i]] for i in idx]
        gr = [[grow_scr[b * LANES + C_GATE_COL + h:b * LANES + C_GATE_COL + h + 1, :] for h in heads[i]]
              for i, (b, _) in enumerate(units)]
        gl = [[gc[i][j][CHUNK - 1:CHUNK, :] for j in range(2)] for i in idx]
        eg = [[jnp.exp(gc[i][j]) for j in range(2)] for i in idx]
        q = [[unit_norm(qkv_scr[rsl[i], head_cols(0, h)], qscale) for h in heads[i]] for i in idx]
        k = [[unit_norm(qkv_scr[rsl[i], head_cols(1, h)], 1.0) for h in heads[i]] for i in idx]
        v = [[qkv_scr[rsl[i], head_cols(2, h)] for h in heads[i]] for i in idx]
        kb = [[k[i][j] * beta[i][j] for j in range(2)] for i in idx]
        lhs = [jnp.concatenate([lane_cat(q[i][0], q[i][1]), lane_cat(kb[i][0], kb[i][1])], axis=0).astype(BF16)
               for i in idx]
        kbd = [block_diag2(k[i][0].astype(BF16), k[i][1].astype(BF16)) for i in idx]
        qk = [_dot_nt(lhs[i], kbd[i]) for i in idx]
        fill()
        gc2 = [jnp.where(lane_lo, gc[i][0], gc[i][1]) for i in idx]
        gr2 = [jnp.where(lane_lo[0:1], gr[i][0], gr[i][1]) for i in idx]
        decay = [jnp.where(tril2, jnp.exp(jnp.where(tril2, gc2[i] - gr2[i], 0.0)), 0.0) for i in idx]
        attn = [(qk[i][0:CHUNK] * decay[i]).astype(BF16) for i in idx]
        lmat = [jnp.where(strict2, qk[i][CHUNK:2 * CHUNK] * decay[i], 0.0) for i in idx]
        tinv = [eye2 - lmat[i] for i in idx]
        pw = [lmat[i].astype(BF16) for i in idx]
        for level in range(5):
            pw = [_dot(pw[i], jnp.concatenate([pw[i] * keep_lo, pw[i] * keep_hi], axis=0)).astype(BF16) for i in idx]
            tinv = [tinv[i] + _dot(tinv[i].astype(BF16), jnp.concatenate([pw[i] * keep_lo, pw[i] * keep_hi], axis=0))
                    for i in idx]
            fill()
        rhs = [jnp.concatenate([
            jnp.concatenate([(v[i][0] * beta[i][0]).astype(BF16), zero_t, (kb[i][0] * eg[i][0]).astype(BF16), zero_t], axis=1),
            jnp.concatenate([zero_t, (v[i][1] * beta[i][1]).astype(BF16), zero_t, (kb[i][1] * eg[i][1]).astype(BF16)], axis=1),
        ], axis=0) for i in idx]
        uw = [_dot(tinv[i].astype(BF16), rhs[i]) for i in idx]
        fill()
        sbd = [block_diag2(st[i][0].astype(BF16), st[i][1].astype(BF16)) for i in idx]
        wq = [_dot(jnp.concatenate([uw[i][:, 2 * C_HEAD_DIM:],
                                    lane_cat(q[i][0] * eg[i][0], q[i][1] * eg[i][1])], axis=0).astype(BF16), sbd[i])
              for i in idx]
        vnb = [(uw[i][:, 0:2 * C_HEAD_DIM] - wq[i][0:CHUNK]).astype(BF16) for i in idx]
        vbd = [block_diag2(vnb[i][:, 0:C_HEAD_DIM], vnb[i][:, C_HEAD_DIM:]) for i in idx]
        o = [wq[i][CHUNK:2 * CHUNK] + _dot(attn[i], vbd[i]) for i in idx]
        fill()
        assert not side and fill_points[0] == 0
        for i, (b, _) in enumerate(units):
            for j, h in enumerate(heads[i]):
                hs = slice(j * C_HEAD_DIM, (j + 1) * C_HEAD_DIM)
                kdec = (k[i][j] * jnp.exp(gl[i][j] - gc[i][j])).astype(BF16)
                state[b * C_HEADS + h] = st[i][j] * jnp.exp(gl[i][j]) + _dot_tn(kdec, vnb[i][:, hs])
                oh = o[i][:, hs]
                on = oh * lax.rsqrt(jnp.mean(oh * oh, axis=-1, keepdims=True) + NORM_EPS) * norm_g
                og_scr[rsl[i], out_cols(h)] = on * _silu(z_scr[rsl[i], out_cols(h)])

    for p0 in range(0, n_pairs, C_PHASE_PAIRS):
        nxt = range(p0 + C_PHASE_PAIRS, min(p0 + 2 * C_PHASE_PAIRS, n_pairs))
        delta_rule_pairs(range(p0, p0 + C_PHASE_PAIRS), [step for p in nxt for step in pair_inputs(p)])

    y = _dot(og_scr[...].astype(BF16), wout_ref[...])
    out = _layer_norm(DN_ALPHA * x + y, lng_ref[...], lnb_ref[...])
    o_ref[...] = out.reshape(bb, CHUNK, d_model)


def _mixer_c(x3, w1, w2, convw, alog, dtb, ng, wout, lng, lnb):
    b, s, d = x3.shape
    bb = min(MIX_BATCH, b)
    assert b % bb == 0 and s % CHUNK == 0
    rows = bb * CHUNK
    tile = pl.BlockSpec((bb, CHUNK, d), lambda i, j: (i, j, 0))
    in_specs = [tile] + [_resident(a.shape) for a in (w1, w2, convw, alog, dtb, ng, wout, lng, lnb)]
    return pl.pallas_call(
        functools.partial(_mixer_c_body, bb=bb),
        out_shape=jax.ShapeDtypeStruct((b, s, d), F32),
        grid=(b // bb, s // CHUNK),
        in_specs=in_specs,
        out_specs=tile,
        scratch_shapes=[
            pltpu.VMEM((bb, SUBLANES, 3 * C_WIDTH), F32),
            pltpu.VMEM((rows, 3 * C_WIDTH), F32),
            pltpu.VMEM((rows, C_WIDTH), F32),
            pltpu.VMEM((rows, LANES), F32),
            pltpu.VMEM((rows, LANES), F32),
            pltpu.VMEM((bb * LANES, LANES), F32),
            pltpu.VMEM((bb * C_HEADS, C_HEAD_DIM, C_HEAD_DIM), F32),
            pltpu.VMEM((rows, C_WIDTH), F32),
        ],
        compiler_params=pltpu.CompilerParams(dimension_semantics=("arbitrary", "arbitrary"),
                                             vmem_limit_bytes=VMEM_LIMIT),
        name="mixer_c",
    )(x3, w1, w2, convw, alog, dtb, ng, wout, lng, lnb)


def _block_diag(w):
    nb, bs, _ = w.shape
    out = jnp.zeros((nb * bs, nb * bs), w.dtype)
    for i in range(nb):
        out = out.at[i * bs:(i + 1) * bs, i * bs:(i + 1) * bs].set(w[i])
    return out


def _pair_major(m):
    pw = 2 * C_HEAD_DIM
    return jnp.concatenate([m[..., part * C_WIDTH + p * pw:part * C_WIDTH + (p + 1) * pw]
                            for p in range(C_HEADS // 2) for part in range(3)], axis=-1)


def _row(v):
    return v.reshape(1, -1).astype(F32)


def kernel(x, p, ffn1_wg, ffn1_wu, ffn1_wd, ffn2_wg, ffn2_wu, ffn2_wd, ln_g, ln_b, ple_wg, ple_bg, ple_wp, ab_w_in, a_sinks, b_conv_w, b_conv_b, b_wa, b_ba, b_wx, b_bx, b_lam, ab_w_out, c_w_in, c_conv_w, c_a_log, c_dt_bias, c_norm_g, c_w_out):
    b, s, d = x.shape
    n = b * s
    ffn1 = [w.astype(BF16) for w in (ffn1_wg, ffn1_wu, ffn1_wd)]
    ffn2 = [w.astype(BF16) for w in (ffn2_wg, ffn2_wu, ffn2_wd)]
    ple_w = (ple_wg.astype(BF16), ple_wp.astype(BF16))
    p_all = p.reshape(DEPTH, n, -1)
    for i in range(DEPTH):
        j = i // 2
        if i % 2 == 0:
            w = ab_w_in[j]
            o1, o2, o3 = A_WIDTH, A_WIDTH + A_KV_WIDTH, A_WIDTH + 2 * A_KV_WIDTH
            hd = A_HEAD_DIM
            swap = lambda m: jnp.concatenate([m[:, hd:2 * hd], m[:, 0:hd]], axis=1)
            win = jnp.concatenate([w[:, :o1], w[:, o1:o2], swap(w[:, o1:o2]), w[:, o2:o3], swap(w[:, o2:o3]),
                                   w[:, o3:]], axis=1).astype(BF16)
            x = _ffn_mixer_ab(x, i, *ffn1, _row(ln_g[i, 0]), _row(ln_b[i, 0]),
                              win, a_sinks[j].astype(F32), b_conv_w[j], _row(b_conv_b[j]),
                              _block_diag(b_wa[j]).astype(BF16), _row(b_ba[j]),
                              _block_diag(b_wx[j]).astype(BF16), _row(b_bx[j]), _row(b_lam[j]),
                              ab_w_out[j].astype(BF16), _row(ln_g[i, 1]), _row(ln_b[i, 1]))
        else:
            x = _ffn(x.reshape(n, d), i, *ffn1, _row(ln_g[i, 0]), _row(ln_b[i, 0])).reshape(b, s, d)
            w = c_w_in[j]
            w1 = jnp.concatenate([_pair_major(w[:, :3 * C_WIDTH]), w[:, 3 * C_WIDTH:4 * C_WIDTH]], axis=1).astype(BF16)
            w2 = jnp.pad(w[:, 4 * C_WIDTH:], ((0, 0), (0, LANES - 2 * C_HEADS))).astype(BF16)
            pad_gate = lambda v: jnp.pad(v.astype(F32), (C_GATE_COL, LANES - C_GATE_COL - C_HEADS)).reshape(1, LANES)
            x = _mixer_c(x, w1, w2, _pair_major(c_conv_w[j]), pad_gate(c_a_log[j]), pad_gate(c_dt_bias[j]),
                         _row(c_norm_g[j]), c_w_out[j].astype(BF16), _row(ln_g[i, 1]), _row(ln_b[i, 1]))
        x = _ffn(x.reshape(n, d), i, *ffn2, _row(ln_g[i, 2]), _row(ln_b[i, 2]),
                 ple_args=(p_all, ple_w[0], _row(ple_bg[i]), ple_w[1])).reshape(b, s, d)
    return x
```

```python
import functools

import jax
import jax.numpy as jnp
from jax import lax
from jax.experimental import pallas as pl
from jax.experimental.pallas import tpu as pltpu

F32 = jnp.float32
BF16 = jnp.bfloat16

DEPTH = 2
CHUNK = 64
A_HEADS = 8
A_KV_HEADS = 2
A_HEAD_DIM = 64
A_WIDTH = A_HEADS * A_HEAD_DIM
A_KV_WIDTH = A_KV_HEADS * A_HEAD_DIM
A_PREV_CHUNKS = 2
A_BAND = (A_PREV_CHUNKS + 1) * CHUNK
B_WIDTH = 512
RG_C = 8.0
C_HEADS = 8
C_HEAD_DIM = 128
C_WIDTH = C_HEADS * C_HEAD_DIM
DN_ALPHA = (2.0 * DEPTH) ** 0.25
LN_EPS = 1e-5
NORM_EPS = 1e-6
NEG = -1e30

LANES = 128
SUBLANES = 8
FFN_ROWS = 1024
FFN_COLS = 256
FFN_OUT_ROWS = 256
MIX_BATCH = 8
A_GROUP_SEQ = 8
C_PHASE_PAIRS = 2
VMEM_LIMIT = 56 * 1024 * 1024


def _dot(a, b):
    return jnp.dot(a, b, preferred_element_type=F32)


def _dot_nt(a, b):
    return lax.dot_general(a, b, (((1,), (1,)), ((), ())), preferred_element_type=F32)


def _dot_tn(a, b):
    return lax.dot_general(a, b, (((0,), (0,)), ((), ())), preferred_element_type=F32)


def _layer_norm(z, g, b):
    mu = jnp.mean(z, axis=-1, keepdims=True)
    d = z - mu
    var = jnp.mean(d * d, axis=-1, keepdims=True)
    return d * lax.rsqrt(var + LN_EPS) * g + b


def _silu(x):
    return x * jax.nn.sigmoid(x)


def _softplus(x):
    e = jnp.exp(-jnp.abs(x))
    u = 1.0 + e
    log1p_e = jnp.where(u == 1.0, e, jnp.log(u) * (e / (u - 1.0)))
    return jnp.maximum(x, 0.0) + log1p_e


def _neg_expm1(y):
    return -jnp.tanh(0.5 * y) * (jnp.exp(y) + 1.0)


def _causal_conv(cur, tail, w):
    taps = w.shape[0]
    b, t, c = cur.shape
    nt = t // SUBLANES
    tiles = jnp.concatenate([tail, cur], axis=1).reshape(b * (nt + 1), SUBLANES, c)
    sub = lax.broadcasted_iota(jnp.int32, (b, nt, SUBLANES, c), 2)
    out = w[taps - 1:taps].reshape(1, 1, -1) * cur
    for d in range(1, taps):
        rot = pltpu.roll(tiles, d, 1).reshape(b, nt + 1, SUBLANES, c)
        shifted = jnp.where(sub < d, rot[:, 0:nt], rot[:, 1:nt + 1]).reshape(b, t, c)
        out = out + w[taps - 1 - d:taps - d].reshape(1, 1, -1) * shifted
    return out


def _resident(shape):
    zeros = (0,) * len(shape)
    return pl.BlockSpec(shape, lambda *_: zeros, pipeline_mode=pl.Buffered(1))


def _tile_rows(ref, start, stop):
    if len(ref.shape) == 2:
        return ref[start:stop, :]
    return ref[start // CHUNK:stop // CHUNK].reshape(stop - start, ref.shape[-1])


def _ffn_steps(x_ref, wg_ref, wu_ref, wd_ref, lng_ref, lnb_ref, h_scr, out_ref):
    rows = h_scr.shape[0]
    cache = {}

    def xb():
        if "xb" not in cache:
            cache["xb"] = _tile_rows(x_ref, 0, rows).astype(BF16)
        return cache["xb"]

    def up(c):
        sl = slice(c * FFN_COLS, (c + 1) * FFN_COLS)
        g = _dot(xb(), wg_ref[:, sl])
        u = _dot(xb(), wu_ref[:, sl])
        h_scr[:, sl] = (_silu(g) * u).astype(BF16)

    def down(r):
        r0, r1 = r * FFN_OUT_ROWS, (r + 1) * FFN_OUT_ROWS
        y = _dot(h_scr[r0:r1, :], wd_ref[...])
        out_ref[r0:r1, :] = _layer_norm(DN_ALPHA * _tile_rows(x_ref, r0, r1) + 0.5 * y, lng_ref[...], lnb_ref[...])

    return ([functools.partial(up, c) for c in range(wg_ref.shape[-1] // FFN_COLS)]
            + [functools.partial(down, r) for r in range(rows // FFN_OUT_ROWS)])


def _ffn_rows(*refs):
    for step in _ffn_steps(*refs):
        step()


def _ffn_body(x_ref, wg_ref, wu_ref, wd_ref, lng_ref, lnb_ref, *rest, ple):
    if ple:
        p_ref, pwg_ref, pbg_ref, pwp_ref, o_ref, h_scr = rest
    else:
        o_ref, h_scr = rest
    _ffn_rows(x_ref, wg_ref, wu_ref, wd_ref, lng_ref, lnb_ref, h_scr, o_ref)
    if ple:
        out = o_ref[...]
        gate = jax.nn.sigmoid(_dot(out.astype(BF16), pwg_ref[...]) + pbg_ref[...])
        o_ref[...] = out + gate * _dot(p_ref[...].astype(BF16), pwp_ref[...])


def _ffn(x2, layer, wg, wu, wd, lng, lnb, ple_args=None):
    n, d = x2.shape
    f = wg.shape[-1]
    rows = min(FFN_ROWS, n)
    assert n % rows == 0 and f % FFN_COLS == 0 and rows % FFN_OUT_ROWS == 0
    row_spec = pl.BlockSpec((rows, d), lambda i: (i, 0))

    def of_layer(shape):
        return pl.BlockSpec((None,) + shape, lambda i: (layer,) + (0,) * len(shape), pipeline_mode=pl.Buffered(1))

    in_specs = [row_spec, of_layer((d, f)), of_layer((d, f)), of_layer((f, d)),
                _resident((1, d)), _resident((1, d))]
    args = [x2, wg, wu, wd, lng, lnb]
    if ple_args is not None:
        p_all, pwg, pbg, pwp = ple_args
        dp = p_all.shape[-1]
        in_specs += [pl.BlockSpec((None, rows, dp), lambda i: (layer, i, 0)), of_layer((d, d)),
                     _resident((1, d)), of_layer((dp, d))]
        args += [p_all, pwg, pbg, pwp]
    return pl.pallas_call(
        functools.partial(_ffn_body, ple=ple_args is not None),
        out_shape=jax.ShapeDtypeStruct((n, d), F32),
        grid=(n // rows,),
        in_specs=in_specs,
        out_specs=row_spec,
        scratch_shapes=[pltpu.VMEM((rows, f), BF16)],
        compiler_params=pltpu.CompilerParams(dimension_semantics=("arbitrary",),
                                             vmem_limit_bytes=VMEM_LIMIT),
        name="ffn_ple" if ple_args is not None else "ffn",
    )(*args)


AB_Q0 = 0
AB_K0 = A_WIDTH
AB_V0 = AB_K0 + 2 * A_KV_WIDTH
AB_X0 = AB_V0 + 2 * A_KV_WIDTH
AB_G0 = AB_X0 + B_WIDTH
AB_COLS = AB_G0 + B_WIDTH


def _mixer_ab_tile(n, reset, x, sinks_ref, win_ref, convw_ref, convb_ref, wa_ref, ba_ref, wx_ref, bx_ref,
                   lam_ref, wout_ref, lng_ref, lnb_ref, proj_scr, kbuf, vbuf, ctail, hcarry, y_scr, *, bb, group,
                   side=()):
    rows = bb * CHUNK
    single_pass = bb == group
    side = list(side)

    def fill(count):
        for _ in range(min(count, len(side))):
            side.pop(0)()

    @pl.when(reset)
    def _():
        kbuf[...] = jnp.zeros_like(kbuf)
        vbuf[...] = jnp.zeros_like(vbuf)
        ctail[...] = jnp.zeros_like(ctail)
        hcarry[...] = jnp.zeros_like(hcarry)

    proj_scr[...] = _dot(x.astype(BF16), win_ref[...])
    fill(2)

    lo = lax.broadcasted_iota(jnp.int32, (rows, LANES), 1) < A_HEAD_DIM

    def masked_variants(c0):
        ta = proj_scr[:, c0:c0 + LANES]
        tb = proj_scr[:, c0 + LANES:c0 + 2 * LANES]
        zero = jnp.zeros_like(ta)
        parts = [jnp.where(lo, ta, zero), jnp.where(lo, zero, tb),
                 jnp.where(lo, tb, zero), jnp.where(lo, zero, ta)]
        return jnp.concatenate(parts, axis=1).astype(BF16).reshape(bb, CHUNK, 4 * LANES)

    old_k = kbuf[:, CHUNK:A_BAND, :]
    kbuf[:, 0:A_BAND - CHUNK, :] = old_k
    kbuf[:, A_BAND - CHUNK:A_BAND, :] = masked_variants(AB_K0)
    old_v = vbuf[:, CHUNK:A_BAND, :]
    vbuf[:, 0:A_BAND - CHUNK, :] = old_v
    vbuf[:, A_BAND - CHUNK:A_BAND, :] = masked_variants(AB_V0)

    qi = lax.broadcasted_iota(jnp.int32, (2 * CHUNK, A_BAND), 0)
    ki = lax.broadcasted_iota(jnp.int32, (2 * CHUNK, A_BAND), 1)
    first_tile = qi < CHUNK
    dist = jnp.abs(jnp.where(first_tile, qi, qi - CHUNK) + (A_BAND - CHUNK) - ki).astype(F32)
    valid = ki >= (A_BAND - CHUNK) - CHUNK * n
    row_first = lax.broadcasted_iota(jnp.int32, (2 * CHUNK, 1), 0) < CHUNK
    scale = A_HEAD_DIM ** -0.5

    def head_consts(kv, parity):
        ha = 4 * kv + parity
        hb = ha + 2
        slope = jnp.where(row_first, 2.0 ** (-8.0 * (ha + 1) / A_HEADS), 2.0 ** (-8.0 * (hb + 1) / A_HEADS))
        sink = jnp.where(row_first, sinks_ref[ha], sinks_ref[hb])
        return slope * dist, sink

    consts = [[head_consts(kv, par) for par in range(2)] for kv in range(A_KV_HEADS)]

    def attend(it, carry):
        combos = [(it * group + sq, kv, par) for sq in range(group) for kv in range(A_KV_HEADS) for par in range(2)]
        idx = range(len(combos))
        rsl = [pl.ds(pl.multiple_of(b * CHUNK, CHUNK), CHUNK) for b, _, _ in combos]
        tile = [slice((2 * kv + par) * LANES, (2 * kv + par + 1) * LANES) for _, kv, par in combos]
        bias = [consts[kv][par][0] for _, kv, par in combos]
        sink = [consts[kv][par][1] for _, kv, par in combos]
        q2 = {}
        for i, (b, kv, par) in enumerate(combos):
            if par == 0:
                c0 = AB_Q0 + 2 * LANES * kv
                qq = jnp.concatenate([proj_scr[rsl[i], c0:c0 + LANES], proj_scr[rsl[i], c0 + LANES:c0 + 2 * LANES]],
                                     axis=0)
                q2[i] = q2[i + 1] = (qq * scale).astype(BF16)
        s = [_dot_nt(q2[i], kbuf[b, :, tile[i]]) for i, (b, _, _) in enumerate(combos)]
        if single_pass:
            fill(3)
        s = [jnp.where(valid, s[i] - bias[i], NEG) for i in idx]
        m = [jnp.maximum(jnp.max(s[i], axis=-1, keepdims=True), sink[i]) for i in idx]
        pr = [jnp.exp(s[i] - m[i]) for i in idx]
        den = [jnp.sum(pr[i], axis=-1, keepdims=True) + jnp.exp(sink[i] - m[i]) for i in idx]
        pv = [_dot(pr[i].astype(BF16), vbuf[b, :, tile[i]]) for i, (b, _, _) in enumerate(combos)]
        if single_pass:
            fill(1)
        for i, (b, kv, par) in enumerate(combos):
            if par == 0:
                c0 = AB_Q0 + 2 * LANES * kv
                acc = pv[i] / den[i] + pv[i + 1] / den[i + 1]
                y_scr[rsl[i], c0:c0 + LANES] = acc[0:CHUNK]
                y_scr[rsl[i], c0 + LANES:c0 + 2 * LANES] = acc[CHUNK:2 * CHUNK]
        return carry

    if single_pass:
        attend(0, 0)
    else:
        lax.fori_loop(0, bb // group, attend, 0)

    bx3 = proj_scr[:, AB_X0:AB_X0 + B_WIDTH].reshape(bb, CHUNK, B_WIDTH)
    conv = _causal_conv(bx3, ctail[...], convw_ref[...]) + convb_ref[...].reshape(1, 1, B_WIDTH)
    ctail[...] = bx3[:, CHUNK - SUBLANES:CHUNK, :]
    c2 = conv.reshape(rows, B_WIDTH)
    cb = c2.astype(BF16)
    r = jax.nn.sigmoid(_dot(cb, wa_ref[...]) + ba_ref[...])
    i = jax.nn.sigmoid(_dot(cb, wx_ref[...]) + bx_ref[...])
    fill(5)
    log_a = (-RG_C) * r * _softplus(-lam_ref[...])
    n_tiles = CHUNK // SUBLANES
    a = jnp.exp(log_a).reshape(bb * n_tiles, SUBLANES, B_WIDTH)
    u = (jnp.sqrt(_neg_expm1(2.0 * log_a)) * (i * c2)).reshape(bb * n_tiles, SUBLANES, B_WIDTH)
    ti = lax.broadcasted_iota(jnp.int32, (bb * n_tiles, SUBLANES, B_WIDTH), 1)
    step = 1
    while step < SUBLANES:
        keep = ti >= step
        a_prev = jnp.where(keep, pltpu.roll(a, step, 1), 1.0)
        u_prev = jnp.where(keep, pltpu.roll(u, step, 1), 0.0)
        u = u + a * u_prev
        a = a * a_prev
        step *= 2
    a = a.reshape(bb, n_tiles, SUBLANES, B_WIDTH)
    u = u.reshape(bb, n_tiles, SUBLANES, B_WIDTH)
    carry = hcarry[...].reshape(bb, 1, B_WIDTH)
    h_tiles = []
    for t in range(n_tiles):
        h_t = u[:, t] + a[:, t] * carry
        carry = h_t[:, SUBLANES - 1:SUBLANES, :]
        h_tiles.append(h_t)
    hcarry[...] = carry.reshape(bb, B_WIDTH)
    h = jnp.concatenate(h_tiles, axis=1)
    gate = jax.nn.gelu(proj_scr[:, AB_G0:AB_G0 + B_WIDTH], approximate=True)
    y_scr[:, A_WIDTH:A_WIDTH + B_WIDTH] = h.reshape(rows, B_WIDTH) * gate

    y = _dot(y_scr[...].astype(BF16), wout_ref[...])
    fill(len(side))
    return _layer_norm(DN_ALPHA * x + y, lng_ref[...], lnb_ref[...])


def _ffn_mixer_ab_body(sinks_ref, x_ref, wg_ref, wu_ref, wd_ref, lng1_ref, lnb1_ref, *rest, bb, group):
    weights, o_ref, h_scr, f1_scr, scratch = rest[:11], rest[11], rest[12], rest[13], rest[14:]
    j = pl.program_id(1)

    @pl.when(j == 0)
    def _():
        f1_scr[...] = jnp.zeros_like(f1_scr)

    prev = f1_scr[(j + 1) % 2]
    ffn = _ffn_steps(x_ref, wg_ref, wu_ref, wd_ref, lng1_ref, lnb1_ref, h_scr, f1_scr.at[j % 2])
    out = _mixer_ab_tile(j - 1, j <= 1, prev, sinks_ref, *weights, *scratch, bb=bb, group=group, side=ffn)
    o_ref[...] = out.reshape(o_ref.shape)


def _ab_scratch(bb):
    rows = bb * CHUNK
    return [
        pltpu.VMEM((rows, AB_COLS), F32),
        pltpu.VMEM((bb, A_BAND, 4 * LANES), BF16),
        pltpu.VMEM((bb, A_BAND, 4 * LANES), BF16),
        pltpu.VMEM((bb, SUBLANES, B_WIDTH), F32),
        pltpu.VMEM((bb, B_WIDTH), F32),
        pltpu.VMEM((rows, A_WIDTH + B_WIDTH), F32),
    ]


def _ffn_mixer_ab(x3, layer, wg, wu, wd, lng1, lnb1, win, sinks, convw, convb, wa, ba, wx, bx, lam, wout, lng, lnb):
    b, s, d = x3.shape
    f = wg.shape[-1]
    bb = min(MIX_BATCH, b)
    group = min(A_GROUP_SEQ, bb)
    n_chunks = s // CHUNK
    rows = bb * CHUNK
    assert b % bb == 0 and s % CHUNK == 0 and bb % group == 0 and rows % FFN_OUT_ROWS == 0 and f % FFN_COLS == 0
    in_tile = pl.BlockSpec((bb, CHUNK, d), lambda i, j: (i, jnp.minimum(j, n_chunks - 1), 0))
    out_tile = pl.BlockSpec((bb, CHUNK, d), lambda i, j: (i, jnp.maximum(j - 1, 0), 0))

    def of_layer(shape):
        return pl.BlockSpec((None,) + shape, lambda i, j: (layer,) + (0,) * len(shape), pipeline_mode=pl.Buffered(1))

    mixer_w = (win, convw, convb, wa, ba, wx, bx, lam, wout, lng, lnb)
    in_specs = ([pl.BlockSpec(memory_space=pltpu.SMEM), in_tile, of_layer((d, f)), of_layer((d, f)),
                 of_layer((f, d)), _resident(lng1.shape), _resident(lnb1.shape)]
                + [_resident(w.shape) for w in mixer_w])
    return pl.pallas_call(
        functools.partial(_ffn_mixer_ab_body, bb=bb, group=group),
        out_shape=jax.ShapeDtypeStruct((b, s, d), F32),
        grid=(b // bb, n_chunks + 1),
        in_specs=in_specs,
        out_specs=out_tile,
        scratch_shapes=[pltpu.VMEM((rows, f), BF16), pltpu.VMEM((2, rows, d), F32)] + _ab_scratch(bb),
        compiler_params=pltpu.CompilerParams(dimension_semantics=("arbitrary", "arbitrary"),
                                             vmem_limit_bytes=VMEM_LIMIT),
        name="ffn_mixer_ab",
    )(sinks, x3, wg, wu, wd, lng1, lnb1, *mixer_w)


C_GATE_COL = C_HEADS


def _mixer_c_body(x_ref, w1_ref, w2_ref, convw_ref, alog_ref, dtb_ref, ng_ref, wout_ref, lng_ref, lnb_ref,
                  o_ref, ctail, stage, qkv_scr, z_scr, ba_scr, gcol_scr, grow_scr, state, og_scr, *, bb):
    n = pl.program_id(1)
    rows = bb * CHUNK
    d_model = x_ref.shape[-1]
    qkv_w = 3 * C_WIDTH

    @pl.when(n == 0)
    def _():
        ctail[...] = jnp.zeros_like(ctail)
        state[...] = jnp.zeros_like(state)

    x = x_ref[...].reshape(rows, d_model)
    xb = x.astype(BF16)
    n_pairs = C_HEADS // 2
    pair_w = 2 * C_HEAD_DIM

    seq_rows = 2 * (SUBLANES + CHUNK)
    taps = convw_ref.shape[0]

    def project_qkv(col0):
        csl = slice(col0, col0 + pair_w)
        cur = _dot(xb, w1_ref[:, csl]).reshape(bb, CHUNK, pair_w)
        tail = ctail[:, :, csl]
        for slab in range(pair_w // LANES):
            lsl = slice(slab * LANES, (slab + 1) * LANES)
            w = convw_ref[:, col0 + slab * LANES:col0 + (slab + 1) * LANES]
            for b in range(bb):
                stage[slab, pl.ds(b * seq_rows, SUBLANES, stride=2), :] = tail[b, :, lsl]
                stage[slab, pl.ds(b * seq_rows + 2 * SUBLANES, CHUNK, stride=2), :] = cur[b, :, lsl]
            for b in range(bb):
                row0 = b * seq_rows + 2 * SUBLANES
                acc = w[taps - 1:taps] * cur[b, :, lsl]
                for d in range(1, taps):
                    acc = acc + w[taps - 1 - d:taps - d] * stage[slab, pl.ds(row0 - 2 * d, CHUNK, stride=2), :]
                qkv_scr[b * CHUNK:(b + 1) * CHUNK, col0 + slab * LANES:col0 + (slab + 1) * LANES] = _silu(acc)
        ctail[:, :, csl] = cur[:, CHUNK - SUBLANES:CHUNK, :]

    def project_z(p):
        z_scr[:, p * pair_w:(p + 1) * pair_w] = _dot(xb, w1_ref[:, qkv_w + p * pair_w:qkv_w + (p + 1) * pair_w])

    def pair_inputs(p):
        return [functools.partial(project_qkv, (3 * p + part) * pair_w) for part in range(3)] + [
            functools.partial(project_z, p)]

    ba_scr[...] = _dot(xb, w2_ref[...])
    for p in range(C_PHASE_PAIRS):
        for step in pair_inputs(p):
            step()

    ri = lax.broadcasted_iota(jnp.int32, (CHUNK, CHUNK), 0)
    ci = lax.broadcasted_iota(jnp.int32, (CHUNK, CHUNK), 1)
    tril = ri >= ci
    ri2 = lax.broadcasted_iota(jnp.int32, (CHUNK, LANES), 0)
    li2 = lax.broadcasted_iota(jnp.int32, (CHUNK, LANES), 1)
    ci2 = jnp.bitwise_and(li2, CHUNK - 1)
    lane_lo = li2 < CHUNK
    tril2 = ri2 >= ci2
    strict2 = ri2 > ci2
    eye2 = (ri2 == ci2).astype(F32)
    keep_lo = jnp.where(lane_lo, 1.0, 0.0).astype(BF16)
    keep_hi = jnp.where(lane_lo, 0.0, 1.0).astype(BF16)
    zero_t = jnp.zeros((CHUNK, C_HEAD_DIM), BF16)
    neg_rate = -jnp.exp(alog_ref[...])
    dtb = dtb_ref[...]
    norm_g = ng_ref[...]
    qscale = C_HEAD_DIM ** -0.5

    bab = ba_scr[...]
    g_all = neg_rate * _softplus(bab + dtb)
    ba_scr[...] = jax.nn.sigmoid(bab)
    g_hi = g_all.astype(BF16)
    rem = g_all - g_hi.astype(F32)
    g_mid = rem.astype(BF16)
    g_lo = (rem - g_mid.astype(F32)).astype(BF16)
    pieces = jnp.concatenate([part[b * CHUNK:(b + 1) * CHUNK] for part in (g_hi, g_mid, g_lo) for b in range(bb)],
                             axis=1)
    csum = _dot(tril.astype(BF16), pieces)
    for b in range(bb):
        gcb = (csum[:, b * LANES:(b + 1) * LANES] + csum[:, (bb + b) * LANES:(bb + b + 1) * LANES]
               + csum[:, (2 * bb + b) * LANES:(2 * bb + b + 1) * LANES])
        gcol_scr[b * CHUNK:(b + 1) * CHUNK, :] = gcb
        grow_scr[b * LANES:(b + 1) * LANES, :] = jnp.concatenate([gcb, gcb], axis=0).T

    def unit_norm(t, extra):
        return t * (lax.rsqrt(jnp.sum(t * t, axis=-1, keepdims=True) + NORM_EPS) * extra)

    def head_cols(part, h):
        c0 = (3 * (h // 2) + part) * pair_w + (h % 2) * C_HEAD_DIM
        return slice(c0, c0 + C_HEAD_DIM)

    def out_cols(h):
        return slice(h * C_HEAD_DIM, (h + 1) * C_HEAD_DIM)

    def lane_cat(a, b):
        return jnp.concatenate([a, b], axis=1)

    def block_diag2(top, bottom):
        return jnp.concatenate([lane_cat(top, jnp.zeros_like(top)), lane_cat(jnp.zeros_like(bottom), bottom)], axis=0)

    def delta_rule_pairs(pairs, side):
        side = list(side)
        fill_points = [8]

        def fill():
            for _ in range(-(-len(side) // fill_points[0])):
                side.pop(0)()
            fill_points[0] -= 1

        units = [(s, pair) for pair in pairs for s in range(bb)]
        idx = range(len(units))
        rsl = [slice(b * CHUNK, (b + 1) * CHUNK) for b, _ in units]
        heads = [(2 * p, 2 * p + 1) for _, p in units]
        st = [[state[b * C_HEADS + h] for h in heads[i]] for i, (b, _) in enumerate(units)]
        beta = [[ba_scr[rsl[i], h:h + 1] for h in heads[i]] for i in idx]
        gc = [[gcol_scr[rsl[i], C_GATE_COL + h:C_GATE_COL + h + 1] for h in heads[i]] for i in idx]
        gr = [[grow_scr[b * LANES + C_GATE_COL + h:b * LANES + C_GATE_COL + h + 1, :] for h in heads[i]]
              for i, (b, _) in enumerate(units)]
        gl = [[gc[i][j][CHUNK - 1:CHUNK, :] for j in range(2)] for i in idx]
        eg = [[jnp.exp(gc[i][j]) for j in range(2)] for i in idx]
        q = [[unit_norm(qkv_scr[rsl[i], head_cols(0, h)], qscale) for h in heads[i]] for i in idx]
        k = [[unit_norm(qkv_scr[rsl[i], head_cols(1, h)], 1.0) for h in heads[i]] for i in idx]
        v = [[qkv_scr[rsl[i], head_cols(2, h)] for h in heads[i]] for i in idx]
        kb = [[k[i][j] * beta[i][j] for j in range(2)] for i in idx]
        lhs = [jnp.concatenate([lane_cat(q[i][0], q[i][1]), lane_cat(kb[i][0], kb[i][1])], axis=0).astype(BF16)
               for i in idx]
        kbd = [block_diag2(k[i][0].astype(BF16), k[i][1].astype(BF16)) for i in idx]
        qk = [_dot_nt(lhs[i], kbd[i]) for i in idx]
        fill()
        gc2 = [jnp.where(lane_lo, gc[i][0], gc[i][1]) for i in idx]
        gr2 = [jnp.where(lane_lo[0:1], gr[i][0], gr[i][1]) for i in idx]
        decay = [jnp.where(tril2, jnp.exp(jnp.where(tril2, gc2[i] - gr2[i], 0.0)), 0.0) for i in idx]
        attn = [(qk[i][0:CHUNK] * decay[i]).astype(BF16) for i in idx]
        lmat = [jnp.where(strict2, qk[i][CHUNK:2 * CHUNK] * decay[i], 0.0) for i in idx]
        tinv = [eye2 - lmat[i] for i in idx]
        pw = [lmat[i].astype(BF16) for i in idx]
        for level in range(5):
            pw = [_dot(pw[i], jnp.concatenate([pw[i] * keep_lo, pw[i] * keep_hi], axis=0)).astype(BF16) for i in idx]
            tinv = [tinv[i] + _dot(tinv[i].astype(BF16), jnp.concatenate([pw[i] * keep_lo, pw[i] * keep_hi], axis=0))
                    for i in idx]
            fill()
        rhs = [jnp.concatenate([
            jnp.concatenate([(v[i][0] * beta[i][0]).astype(BF16), zero_t, (kb[i][0] * eg[i][0]).astype(BF16), zero_t], axis=1),
            jnp.concatenate([zero_t, (v[i][1] * beta[i][1]).astype(BF16), zero_t, (kb[i][1] * eg[i][1]).astype(BF16)], axis=1),
        ], axis=0) for i in idx]
        uw = [_dot(tinv[i].astype(BF16), rhs[i]) for i in idx]
        fill()
        sbd = [block_diag2(st[i][0].astype(BF16), st[i][1].astype(BF16)) for i in idx]
        wq = [_dot(jnp.concatenate([uw[i][:, 2 * C_HEAD_DIM:],
                                    lane_cat(q[i][0] * eg[i][0], q[i][1] * eg[i][1])], axis=0).astype(BF16), sbd[i])
              for i in idx]
        vnb = [(uw[i][:, 0:2 * C_HEAD_DIM] - wq[i][0:CHUNK]).astype(BF16) for i in idx]
        vbd = [block_diag2(vnb[i][:, 0:C_HEAD_DIM], vnb[i][:, C_HEAD_DIM:]) for i in idx]
        o = [wq[i][CHUNK:2 * CHUNK] + _dot(attn[i], vbd[i]) for i in idx]
        fill()
        assert not side and fill_points[0] == 0
        for i, (b, _) in enumerate(units):
            for j, h in enumerate(heads[i]):
                hs = slice(j * C_HEAD_DIM, (j + 1) * C_HEAD_DIM)
                kdec = (k[i][j] * jnp.exp(gl[i][j] - gc[i][j])).astype(BF16)
                state[b * C_HEADS + h] = st[i][j] * jnp.exp(gl[i][j]) + _dot_tn(kdec, vnb[i][:, hs])
                oh = o[i][:, hs]
                on = oh * lax.rsqrt(jnp.mean(oh * oh, axis=-1, keepdims=True) + NORM_EPS) * norm_g
                og_scr[rsl[i], out_cols(h)] = on * _silu(z_scr[rsl[i], out_cols(h)])

    for p0 in range(0, n_pairs, C_PHASE_PAIRS):
        nxt = range(p0 + C_PHASE_PAIRS, min(p0 + 2 * C_PHASE_PAIRS, n_pairs))
        delta_rule_pairs(range(p0, p0 + C_PHASE_PAIRS), [step for p in nxt for step in pair_inputs(p)])

    y = _dot(og_scr[...].astype(BF16), wout_ref[...])
    out = _layer_norm(DN_ALPHA * x + y, lng_ref[...], lnb_ref[...])
    o_ref[...] = out.reshape(bb, CHUNK, d_model)


def _mixer_c(x3, w1, w2, convw, alog, dtb, ng, wout, lng, lnb):
    b, s, d = x3.shape
    bb = min(MIX_BATCH, b)
    assert b % bb == 0 and s % CHUNK == 0
    rows = bb * CHUNK
    tile = pl.BlockSpec((bb, CHUNK, d), lambda i, j: (i, j, 0))
    in_specs = [tile] + [_resident(a.shape) for a in (w1, w2, convw, alog, dtb, ng, wout, lng, lnb)]
    return pl.pallas_call(
        functools.partial(_mixer_c_body, bb=bb),
        out_shape=jax.ShapeDtypeStruct((b, s, d), F32),
        grid=(b // bb, s // CHUNK),
        in_specs=in_specs,
        out_specs=tile,
        scratch_shapes=[
            pltpu.VMEM((bb, SUBLANES, 3 * C_WIDTH), F32),
            pltpu.VMEM((2 * C_HEAD_DIM // LANES, bb * 2 * (SUBLANES + CHUNK), LANES), F32),
            pltpu.VMEM((rows, 3 * C_WIDTH), F32),
            pltpu.VMEM((rows, C_WIDTH), F32),
            pltpu.VMEM((rows, LANES), F32),
            pltpu.VMEM((rows, LANES), F32),
            pltpu.VMEM((bb * LANES, LANES), F32),
            pltpu.VMEM((bb * C_HEADS, C_HEAD_DIM, C_HEAD_DIM), F32),
            pltpu.VMEM((rows, C_WIDTH), F32),
        ],
        compiler_params=pltpu.CompilerParams(dimension_semantics=("arbitrary", "arbitrary"),
                                             vmem_limit_bytes=VMEM_LIMIT),
        name="mixer_c",
    )(x3, w1, w2, convw, alog, dtb, ng, wout, lng, lnb)


def _block_diag(w):
    nb, bs, _ = w.shape
    out = jnp.zeros((nb * bs, nb * bs), w.dtype)
    for i in range(nb):
        out = out.at[i * bs:(i + 1) * bs, i * bs:(i + 1) * bs].set(w[i])
    return out


def _pair_major(m):
    pw = 2 * C_HEAD_DIM
    return jnp.concatenate([m[..., part * C_WIDTH + p * pw:part * C_WIDTH + (p + 1) * pw]
                            for p in range(C_HEADS // 2) for part in range(3)], axis=-1)


def _row(v):
    return v.reshape(1, -1).astype(F32)


def kernel(x, p, ffn1_wg, ffn1_wu, ffn1_wd, ffn2_wg, ffn2_wu, ffn2_wd, ln_g, ln_b, ple_wg, ple_bg, ple_wp, ab_w_in, a_sinks, b_conv_w, b_conv_b, b_wa, b_ba, b_wx, b_bx, b_lam, ab_w_out, c_w_in, c_conv_w, c_a_log, c_dt_bias, c_norm_g, c_w_out):
    b, s, d = x.shape
    n = b * s
    ffn1 = [w.astype(BF16) for w in (ffn1_wg, ffn1_wu, ffn1_wd)]
    ffn2 = [w.astype(BF16) for w in (ffn2_wg, ffn2_wu, ffn2_wd)]
    ple_w = (ple_wg.astype(BF16), ple_wp.astype(BF16))
    p_all = p.reshape(DEPTH, n, -1)
    for i in range(DEPTH):
        j = i // 2
        if i % 2 == 0:
            w = ab_w_in[j]
            o1, o2, o3 = A_WIDTH, A_WIDTH + A_KV_WIDTH, A_WIDTH + 2 * A_KV_WIDTH
            hd = A_HEAD_DIM
            swap = lambda m: jnp.concatenate([m[:, hd:2 * hd], m[:, 0:hd]], axis=1)
            win = jnp.concatenate([w[:, :o1], w[:, o1:o2], swap(w[:, o1:o2]), w[:, o2:o3], swap(w[:, o2:o3]),
                                   w[:, o3:]], axis=1).astype(BF16)
            x = _ffn_mixer_ab(x, i, *ffn1, _row(ln_g[i, 0]), _row(ln_b[i, 0]),
                              win, a_sinks[j].astype(F32), b_conv_w[j], _row(b_conv_b[j]),
                              _block_diag(b_wa[j]).astype(BF16), _row(b_ba[j]),
                              _block_diag(b_wx[j]).astype(BF16), _row(b_bx[j]), _row(b_lam[j]),
                              ab_w_out[j].astype(BF16), _row(ln_g[i, 1]), _row(ln_b[i, 1]))
        else:
            x = _ffn(x.reshape(n, d), i, *ffn1, _row(ln_g[i, 0]), _row(ln_b[i, 0])).reshape(b, s, d)
            w = c_w_in[j]
            w1 = jnp.concatenate([_pair_major(w[:, :3 * C_WIDTH]), w[:, 3 * C_WIDTH:4 * C_WIDTH]], axis=1).astype(BF16)
            w2 = jnp.pad(w[:, 4 * C_WIDTH:], ((0, 0), (0, LANES - 2 * C_HEADS))).astype(BF16)
            pad_gate = lambda v: jnp.pad(v.astype(F32), (C_GATE_COL, LANES - C_GATE_COL - C_HEADS)).reshape(1, LANES)
            x = _mixer_c(x, w1, w2, _pair_major(c_conv_w[j]), pad_gate(c_a_log[j]), pad_gate(c_dt_bias[j]),
                         _row(c_norm_g[j]), c_w_out[j].astype(BF16), _row(ln_g[i, 1]), _row(ln_b[i, 1]))
        x = _ffn(x.reshape(n, d), i, *ffn2, _row(ln_g[i, 2]), _row(ln_b[i, 2]),
                 ple_args=(p_all, ple_w[0], _row(ple_bg[i]), ple_w[1])).reshape(b, s, d)
    return x
```
